```python
import math
import jax, jax.numpy as jnp
from jax import lax
import numpy as np


D_MODEL = 2048
BATCH = 2
SEQ = 4096
DEPTH = 4
DEC_BATCH = 8
DEC_SEQ = 8
PAST_LEN = 16384
PAGE_SIZE = 128

D_MIX = D_MODEL
D_ATTN = D_MIX // 2
D_POOL = D_MIX // 4
D_GMLP = D_MIX // 4
N_HEADS_A = 8
DV = D_ATTN // N_HEADS_A
DK = DV // 2
POOL_WINDOWS = (2, 4, 8, 16)
N_POOL_GROUPS = len(POOL_WINDOWS)
POOL_CH = D_POOL // N_POOL_GROUPS
POOL_BUF = max(POOL_WINDOWS) - 1
CHUNK = 128
N_HEADS_C = 4
CH_C = D_GMLP // N_HEADS_C
D_IN = 3 * D_ATTN + D_POOL + 2 * D_GMLP
SPLITS = (D_ATTN, 2 * D_ATTN, 3 * D_ATTN, 3 * D_ATTN + D_POOL)
D_FF = ((8 * D_MODEL // 3 + 255) // 256) * 256
ROPE_THETA = 10000.0
Q_BLOCK = 128
EPS = 1e-6
ATTN_SCALE = DK ** -0.5

kernel_name = 'hybrid_diffattn_pool_gmlp_step'


def rms_norm(x, g):
    xf = x.astype(jnp.float32)
    y = xf * lax.rsqrt(jnp.mean(xf * xf, axis=-1, keepdims=True) + EPS)
    return (y * g.astype(jnp.float32)).astype(x.dtype)


def swiglu(x, w_gate, w_up, w_down):
    return (jax.nn.silu(x @ w_gate) * (x @ w_up)) @ w_down


def rope(x, pos):
    half = DK // 2
    inv_freq = jnp.power(ROPE_THETA, -jnp.arange(half, dtype=jnp.float32) / half)
    ang = pos.astype(jnp.float32)[:, None] * inv_freq[None, :]
    cos = jnp.cos(ang)[None, :, None, None, :]
    sin = jnp.sin(ang)[None, :, None, None, :]
    xf = x.astype(jnp.float32)
    x1, x2 = xf[..., :half], xf[..., half:]
    return jnp.concatenate([x1 * cos - x2 * sin, x2 * cos + x1 * sin], axis=-1).astype(x.dtype)


def project(xn, w_in, q_g, k_g, gmlp_g, pos):
    b, s, _ = xn.shape
    z = xn @ w_in
    q, k, v, u_pool, z_g = jnp.split(z, SPLITS, axis=-1)
    q = rope(rms_norm(q.reshape(b, s, N_HEADS_A, 2, DK), q_g), pos)
    k = rope(rms_norm(k.reshape(b, s, N_HEADS_A, 2, DK), k_g), pos)
    v = v.reshape(b, s, N_HEADS_A, DV)
    z_g = jax.nn.gelu(z_g, approximate=False)
    u_g, v_g = jnp.split(z_g, 2, axis=-1)
    v_g = rms_norm(v_g, gmlp_g)
    return q, k, v, u_pool, u_g, v_g


def diff_lambda(lq1, lk1, lq2, lk2, l_init):
    f = jnp.float32
    return (jnp.exp(jnp.sum(lq1.astype(f) * lk1.astype(f)))
            - jnp.exp(jnp.sum(lq2.astype(f) * lk2.astype(f))) + l_init)


def diff_attn_prompt(q, k, v, lam):
    b, s = q.shape[0], q.shape[1]
    nb = s // Q_BLOCK
    qb = jnp.moveaxis(q.reshape(b, nb, Q_BLOCK, N_HEADS_A, 2, DK), 1, 0)
    kpos = jnp.arange(s, dtype=jnp.int32)

    def block(args):
        qi, i = args
        sc = jnp.einsum('bqhcd,bkhcd->bhcqk', qi, k,
                        preferred_element_type=jnp.float32) * ATTN_SCALE
        qpos = i * Q_BLOCK + jnp.arange(Q_BLOCK, dtype=jnp.int32)
        mask = kpos[None, :] <= qpos[:, None]
        p = jax.nn.softmax(jnp.where(mask, sc, -jnp.inf), axis=-1)
        w = (p[:, :, 0] - lam * p[:, :, 1]).astype(v.dtype)
        return jnp.einsum('bhqk,bkhd->bqhd', w, v)

    o = lax.map(block, (qb, jnp.arange(nb, dtype=jnp.int32)))
    return jnp.moveaxis(o, 0, 1).reshape(b, s, N_HEADS_A, DV)


def diff_attn_sample(q, k_new, v_new, k_past, v_past, lam):
    t = q.shape[1]
    past = k_past.shape[1]
    s_past = jnp.einsum('bqhcd,bkhcd->bhcqk', q, k_past,
                        preferred_element_type=jnp.float32) * ATTN_SCALE
    s_new = jnp.einsum('bqhcd,bkhcd->bhcqk', q, k_new,
                       preferred_element_type=jnp.float32) * ATTN_SCALE
    causal = jnp.tril(jnp.ones((t, t), dtype=bool))
    s_new = jnp.where(causal, s_new, -jnp.inf)
    p = jax.nn.softmax(jnp.concatenate([s_past, s_new], axis=-1), axis=-1)
    w = (p[:, :, 0] - lam * p[:, :, 1]).astype(v_new.dtype)
    return (jnp.einsum('bhqk,bkhd->bqhd', w[..., :past], v_past)
            + jnp.einsum('bhqk,bkhd->bqhd', w[..., past:], v_new))


def diff_out(o, subln_g, l_init):
    b, s = o.shape[0], o.shape[1]
    return (rms_norm(o, subln_g) * (1.0 - l_init)).reshape(b, s, D_ATTN)


def pool_mix(u, pos, pool_w, pool_scale):
    b, l, _ = u.shape
    uf = u.astype(jnp.float32).reshape(b, l, N_POOL_GROUPS, POOL_CH)
    cs = jnp.cumsum(uf, axis=1)
    means = []
    for g, w in enumerate(POOL_WINDOWS):
        csg = cs[:, :, g]
        prev = jnp.pad(csg, ((0, 0), (w, 0), (0, 0)))[:, :l]
        cnt = jnp.minimum(w, pos + 1).astype(jnp.float32)[None, :, None]
        means.append((csg - prev) / cnt)
    d = (jnp.stack(means, axis=2) - uf).astype(u.dtype)
    y = jnp.einsum('blgc,gcd->blgd', d, pool_w).reshape(b, l, D_POOL)
    return y * pool_scale


def gmlp_prompt(u, v, ws, bias):
    b, s, _ = u.shape
    ws_c = jnp.where(jnp.tril(jnp.ones((CHUNK, CHUNK), dtype=bool)), ws, 0.0).astype(ws.dtype)
    vc = v.reshape(b, s // CHUNK, CHUNK, N_HEADS_C, CH_C)
    g = jnp.einsum('hts,bnshc->bnthc', ws_c, vc) + bias.T[None, None, :, :, None]
    return u * g.reshape(b, s, D_GMLP)


def gmlp_sample(u, v, ws, bias):
    b, t, _ = u.shape
    ws_c = jnp.where(jnp.tril(jnp.ones((CHUNK, CHUNK), dtype=bool)), ws, 0.0).astype(ws.dtype)[:, :t, :t]
    vh = v.reshape(b, t, N_HEADS_C, CH_C)
    g = jnp.einsum('hts,bshc->bthc', ws_c, vh) + bias[:, :t].T[None, :, :, None]
    return u * g.reshape(b, t, D_GMLP)


def setup_inputs(seed: int = 0) -> dict:
    key = jax.random.key(seed)
    ks = jax.random.split(key, 40)
    f32 = jnp.float32
    n_pages = PAST_LEN // PAGE_SIZE
    used = DEC_BATCH * n_pages
    n_pool = max((used * 5) // 4, used + 1)

    def nrm(k, shape, scale):
        return jax.random.normal(k, shape, f32) * scale

    def gain(k, shape):
        return 1.0 + 0.05 * jax.random.normal(k, shape, f32)

    page_table = jax.random.permutation(ks[5], n_pool)[:used].reshape(DEC_BATCH, n_pages).astype(jnp.int32)
    return {
        'x_prompt': nrm(ks[0], (BATCH, SEQ, D_MODEL), 1.0),
        'x_sample': nrm(ks[1], (DEC_BATCH, DEC_SEQ, D_MODEL), 1.0),
        'cache_k': nrm(ks[2], (DEPTH, n_pool, PAGE_SIZE, N_HEADS_A, 2 * DK), 1.0),
        'cache_v': nrm(ks[3], (DEPTH, n_pool, PAGE_SIZE, N_HEADS_A, DV), 1.0),
        'state_pool': nrm(ks[4], (DEPTH, DEC_BATCH, POOL_BUF, D_POOL), 1.0),
        'page_table': page_table,
        'norm_ffn1': gain(ks[6], (DEPTH, D_MODEL)),
        'w1_gate': nrm(ks[7], (DEPTH, D_MODEL, D_FF), D_MODEL ** -0.5),
        'w1_up': nrm(ks[8], (DEPTH, D_MODEL, D_FF), D_MODEL ** -0.5),
        'w1_down': nrm(ks[9], (DEPTH, D_FF, D_MODEL), D_FF ** -0.5),
        'norm_mix': gain(ks[10], (DEPTH, D_MODEL)),
        'w_in': nrm(ks[11], (DEPTH, D_MODEL, D_IN), D_MODEL ** -0.5),
        'w_out': nrm(ks[12], (DEPTH, D_MIX, D_MODEL), D_MIX ** -0.5),
        'q_norm': gain(ks[13], (DEPTH, DK)),
        'k_norm': gain(ks[14], (DEPTH, DK)),
        'lambda_q1': nrm(ks[15], (DEPTH, DK), 0.1),
        'lambda_k1': nrm(ks[16], (DEPTH, DK), 0.1),
        'lambda_q2': nrm(ks[17], (DEPTH, DK), 0.1),
        'lambda_k2': nrm(ks[18], (DEPTH, DK), 0.1),
        'subln': gain(ks[19], (DEPTH, DV)),
        'pool_w': nrm(ks[20], (DEPTH, N_POOL_GROUPS, POOL_CH, POOL_CH), POOL_CH ** -0.5),
        'pool_scale': gain(ks[21], (DEPTH, D_POOL)),
        'gmlp_norm': gain(ks[22], (DEPTH, D_GMLP)),
        'gmlp_ws': nrm(ks[23], (DEPTH, N_HEADS_C, CHUNK, CHUNK), CHUNK ** -0.5),
        'gmlp_b': gain(ks[24], (DEPTH, N_HEADS_C, CHUNK)),
        'norm_ffn2': gain(ks[25], (DEPTH, D_MODEL)),
        'w2_gate': nrm(ks[26], (DEPTH, D_MODEL, D_FF), D_MODEL ** -0.5),
        'w2_up': nrm(ks[27], (DEPTH, D_MODEL, D_FF), D_MODEL ** -0.5),
        'w2_down': nrm(ks[28], (DEPTH, D_FF, D_MODEL), D_FF ** -0.5),
    }


def reference(x_prompt, x_sample, cache_k, cache_v, state_pool, page_table,
              norm_ffn1, w1_gate, w1_up, w1_down, norm_mix, w_in, w_out,
              q_norm, k_norm, lambda_q1, lambda_k1, lambda_q2, lambda_k2, subln,
              pool_w, pool_scale, gmlp_norm, gmlp_ws, gmlp_b,
              norm_ffn2, w2_gate, w2_up, w2_down):
    b, seq, _ = x_prompt.shape
    db, dec_seq, _ = x_sample.shape
    n_pages = page_table.shape[1]
    past = n_pages * PAGE_SIZE
    pos_p = jnp.arange(seq, dtype=jnp.int32)
    pos_s = past + jnp.arange(dec_seq, dtype=jnp.int32)
    pos_ext = past - POOL_BUF + jnp.arange(POOL_BUF + dec_seq, dtype=jnp.int32)

    hp, hs = x_prompt, x_sample
    kp_l, vp_l, poolp_l, ks_l, vs_l, pools_l, gv_l = [], [], [], [], [], [], []
    for l in range(DEPTH):
        l_init = 0.8 - 0.6 * math.exp(-0.3 * l)
        lam = diff_lambda(lambda_q1[l], lambda_k1[l], lambda_q2[l], lambda_k2[l], l_init)

        hp = hp + 0.5 * swiglu(rms_norm(hp, norm_ffn1[l]), w1_gate[l], w1_up[l], w1_down[l])
        hs = hs + 0.5 * swiglu(rms_norm(hs, norm_ffn1[l]), w1_gate[l], w1_up[l], w1_down[l])

        q, k, v, up, ug, vg = project(rms_norm(hp, norm_mix[l]), w_in[l], q_norm[l], k_norm[l], gmlp_norm[l], pos_p)
        a = diff_out(diff_attn_prompt(q, k, v, lam), subln[l], l_init)
        pm = pool_mix(up, pos_p, pool_w[l], pool_scale[l])
        gm = gmlp_prompt(ug, vg, gmlp_ws[l], gmlp_b[l])
        hp = hp + jnp.concatenate([a, pm, gm], axis=-1) @ w_out[l]
        kp_l.append(k.reshape(b, seq // PAGE_SIZE, PAGE_SIZE, N_HEADS_A, 2 * DK))
        vp_l.append(v.reshape(b, seq // PAGE_SIZE, PAGE_SIZE, N_HEADS_A, DV))
        poolp_l.append(up[:, seq - POOL_BUF:])

        q, k, v, us, ug, vg = project(rms_norm(hs, norm_mix[l]), w_in[l], q_norm[l], k_norm[l], gmlp_norm[l], pos_s)
        k_past = cache_k[l, page_table].reshape(db, past, N_HEADS_A, 2, DK)
        v_past = cache_v[l, page_table].reshape(db, past, N_HEADS_A, DV)
        a = diff_out(diff_attn_sample(q, k, v, k_past, v_past, lam), subln[l], l_init)
        ext = jnp.concatenate([state_pool[l].astype(us.dtype), us], axis=1)
        pm = pool_mix(ext, pos_ext, pool_w[l], pool_scale[l])[:, POOL_BUF:]
        gm = gmlp_sample(ug, vg, gmlp_ws[l], gmlp_b[l])
        hs = hs + jnp.concatenate([a, pm, gm], axis=-1) @ w_out[l]
        ks_l.append(k.reshape(db, dec_seq, N_HEADS_A, 2 * DK))
        vs_l.append(v)
        pools_l.append(ext[:, dec_seq:])
        gv_l.append(vg)

        hp = hp + 0.5 * swiglu(rms_norm(hp, norm_ffn2[l]), w2_gate[l], w2_up[l], w2_down[l])
        hs = hs + 0.5 * swiglu(rms_norm(hs, norm_ffn2[l]), w2_gate[l], w2_up[l], w2_down[l])

    new_k_prompt = jnp.stack(kp_l)
    new_v_prompt = jnp.stack(vp_l)
    new_pool_prompt = jnp.stack(poolp_l)
    new_k_sample = jnp.stack(ks_l)
    new_v_sample = jnp.stack(vs_l)
    new_pool_sample = jnp.stack(pools_l)
    new_gmlp_v_sample = jnp.stack(gv_l)
    return (hp, hs, new_k_prompt, new_v_prompt, new_pool_prompt, new_k_sample, new_v_sample, new_pool_sample, new_gmlp_v_sample)
```

```python
import functools
import math

import jax
import jax.numpy as jnp
from jax import lax
from jax.experimental import pallas as pl
from jax.experimental.pallas import tpu as pltpu

F32 = jnp.float32
BF16 = jnp.bfloat16

EPS = 1e-6
ROPE_THETA = 10000.0
N_HEADS_A = 8
DV = 128
DK = 64
POOL_WINDOWS = (2, 4, 8, 16)
POOL_BUF = 15
HALO = 16
CHUNK = 128
N_HEADS_C = 4
PAGE_SIZE = 128
LANES = 128
NEG = -1e30
VMEM_LIMIT = 56 * 1024 * 1024

_NT = (((1,), (1,)), ((), ()))


def _dot(a, b):
    return jnp.dot(a, b, preferred_element_type=F32)


def _dot_nt(a, b):
    return lax.dot_general(a, b, _NT, preferred_element_type=F32)


def _rms(x, g):
    ms = jnp.mean(x * x, axis=-1, keepdims=True)
    return x * lax.rsqrt(ms + EPS) * g


def _gelu(x):
    return 0.5 * x * (1.0 + lax.erf(x * (2.0 ** -0.5)))


def _l_init(layer):
    return 0.8 - 0.6 * math.exp(-0.3 * layer)


def _lam(lam_ref, l_init):
    lp = lam_ref[...]
    s1 = jnp.sum(lp[0:1] * lp[1:2], axis=-1, keepdims=True)
    s2 = jnp.sum(lp[2:3] * lp[3:4], axis=-1, keepdims=True)
    return jnp.exp(s1) - jnp.exp(s2) + l_init


def _ffn_kernel(x_ref, g_ref, wg_ref, wu_ref, wd_ref, o_ref, xn_ref, acc_ref):
    j = pl.program_id(1)

    @pl.when(j == 0)
    def _():
        xn_ref[...] = _rms(x_ref[...], g_ref[...]).astype(BF16)
        acc_ref[...] = jnp.zeros_like(acc_ref)

    xn = xn_ref[...]
    gate = _dot(xn, wg_ref[...])
    up = _dot(xn, wu_ref[...])
    act = (gate * jax.nn.sigmoid(gate) * up).astype(BF16)
    acc_ref[...] += _dot(act, wd_ref[...])

    @pl.when(j == pl.num_programs(1) - 1)
    def _():
        o_ref[...] = x_ref[...] + 0.5 * acc_ref[...]


def _ffn(x, g, wg, wu, wd, layer, tm, tf):
    m, d = x.shape
    f = wg.shape[-1]
    return pl.pallas_call(
        _ffn_kernel,
        grid=(m // tm, f // tf),
        in_specs=[
            pl.BlockSpec((tm, d), lambda i, j: (i, 0)),
            pl.BlockSpec((None, 1, d), lambda i, j: (layer, 0, 0)),
            pl.BlockSpec((None, d, tf), lambda i, j: (layer, 0, j)),
            pl.BlockSpec((None, d, tf), lambda i, j: (layer, 0, j)),
            pl.BlockSpec((None, tf, d), lambda i, j: (layer, j, 0)),
        ],
        out_specs=pl.BlockSpec((tm, d), lambda i, j: (i, 0)),
        out_shape=jax.ShapeDtypeStruct((m, d), F32),
        scratch_shapes=[pltpu.VMEM((tm, d), BF16), pltpu.VMEM((tm, d), F32)],
        compiler_params=pltpu.CompilerParams(
            dimension_semantics=("parallel", "arbitrary"), vmem_limit_bytes=VMEM_LIMIT),
        name="ffn",
    )(x, g, wg, wu, wd)


def _proj_kernel(x_ref, g_ref, w_ref, qg_ref, kg_ref, gg_ref, cos_ref, sin_ref, avg_ref,
                 q_ref, k_ref, kb_ref, v_ref, vb_ref, up_ref, ug_ref, vg_ref,
                 *, d_attn, d_pool, d_gmlp, attn_scale):
    xn = _rms(x_ref[...], g_ref[...]).astype(BF16)
    tm = xn.shape[0]
    cos = cos_ref[...]
    sin = sin_ref[...]
    avg = avg_ref[...]
    lane = lax.broadcasted_iota(jnp.int32, (tm, LANES), 1)
    first_half = (lane % DK) < (DK // 2)

    def norm_rope(z, gain):
        z2 = z * z
        hi = z2.astype(BF16)
        lo = (z2 - hi.astype(F32)).astype(BF16)
        ms = _dot(hi, avg) + _dot(lo, avg)
        y = z * lax.rsqrt(ms + EPS) * gain
        partner = jnp.where(first_half, pltpu.roll(y, LANES - DK // 2, 1), pltpu.roll(y, DK // 2, 1))
        return y * cos + partner * sin

    zq = _dot(xn, w_ref[:, 0:d_attn])
    qg = qg_ref[...]
    for c in range(d_attn // LANES):
        sl = slice(c * LANES, (c + 1) * LANES)
        q_ref[:, sl] = (norm_rope(zq[:, sl], qg) * attn_scale).astype(BF16)

    zk = _dot(xn, w_ref[:, d_attn:2 * d_attn])
    kg = kg_ref[...]
    for c in range(d_attn // LANES):
        sl = slice(c * LANES, (c + 1) * LANES)
        kr = norm_rope(zk[:, sl], kg)
        k_ref[:, sl] = kr
        kb_ref[:, sl] = kr.astype(BF16)

    zv = _dot(xn, w_ref[:, 2 * d_attn:3 * d_attn])
    v_ref[...] = zv
    vb_ref[...] = zv.astype(BF16)

    o = 3 * d_attn
    up_ref[...] = _dot(xn, w_ref[:, o:o + d_pool])
    o += d_pool
    ug_ref[...] = _gelu(_dot(xn, w_ref[:, o:o + d_gmlp]))
    o += d_gmlp
    zg = _gelu(_dot(xn, w_ref[:, o:o + d_gmlp]))
    vg_ref[...] = _rms(zg, gg_ref[...])


def _proj(x, g, w_in, qg, kg, gg, cos, sin, avg, layer, tm, table_tiles):
    m, d = x.shape
    d_in = w_in.shape[-1]
    d_attn = N_HEADS_A * DV
    d_pool = 4 * LANES
    d_gmlp = N_HEADS_C * CHUNK
    row = lambda i: (i, 0)
    tab = lambda i: (i % table_tiles, 0)
    par = lambda i: (layer, 0, 0)
    kern = functools.partial(_proj_kernel, d_attn=d_attn, d_pool=d_pool, d_gmlp=d_gmlp,
                             attn_scale=DK ** -0.5)
    return pl.pallas_call(
        kern,
        grid=(m // tm,),
        in_specs=[
            pl.BlockSpec((tm, d), row),
            pl.BlockSpec((None, 1, d), par),
            pl.BlockSpec((None, d, d_in), par),
            pl.BlockSpec((None, 1, LANES), par),
            pl.BlockSpec((None, 1, LANES), par),
            pl.BlockSpec((None, 1, d_gmlp), par),
            pl.BlockSpec((tm, LANES), tab),
            pl.BlockSpec((tm, LANES), tab),
            pl.BlockSpec((LANES, LANES), lambda i: (0, 0)),
        ],
        out_specs=[
            pl.BlockSpec((tm, d_attn), row),
            pl.BlockSpec((tm, d_attn), row),
            pl.BlockSpec((tm, d_attn), row),
            pl.BlockSpec((tm, d_attn), row),
            pl.BlockSpec((tm, d_attn), row),
            pl.BlockSpec((tm, d_pool), row),
            pl.BlockSpec((tm, d_gmlp), row),
            pl.BlockSpec((tm, d_gmlp), row),
        ],
        out_shape=[
            jax.ShapeDtypeStruct((m, d_attn), BF16),
            jax.ShapeDtypeStruct((m, d_attn), F32),
            jax.ShapeDtypeStruct((m, d_attn), BF16),
            jax.ShapeDtypeStruct((m, d_attn), F32),
            jax.ShapeDtypeStruct((m, d_attn), BF16),
            jax.ShapeDtypeStruct((m, d_pool), F32),
            jax.ShapeDtypeStruct((m, d_gmlp), F32),
            jax.ShapeDtypeStruct((m, d_gmlp), F32),
        ],
        compiler_params=pltpu.CompilerParams(
            dimension_semantics=("parallel",), vmem_limit_bytes=VMEM_LIMIT),
        name="proj",
    )(x, g, w_in, qg, kg, gg, cos, sin, avg)


def _attn_kernel(lam_ref, q_ref, k_ref, v_ref, sub_ref, o_ref, m_ref, l_ref, acc_ref,
                 *, t, l_init):
    qi = pl.program_id(2)
    q = q_ref[...]
    lane = lax.broadcasted_iota(jnp.int32, q.shape, 1)
    zero = jnp.zeros_like(q)
    q2 = jnp.concatenate([jnp.where(lane < DK, q, zero), jnp.where(lane >= DK, q, zero)], axis=0)

    m_ref[...] = jnp.full_like(m_ref, NEG)
    l_ref[...] = jnp.zeros_like(l_ref)
    acc_ref[...] = jnp.zeros_like(acc_ref)

    def step(j, masked):
        off = pl.multiple_of(j * t, t)
        kj = k_ref[pl.ds(off, t), :]
        vj = v_ref[pl.ds(off, t), :]
        s = _dot_nt(q2, kj)
        if masked:
            r = lax.broadcasted_iota(jnp.int32, s.shape, 0)
            c = lax.broadcasted_iota(jnp.int32, s.shape, 1)
            s = jnp.where(c <= jnp.where(r >= t, r - t, r), s, NEG)
        m_old = m_ref[...]
        m_new = jnp.maximum(m_old, jnp.max(s, axis=-1, keepdims=True))
        p = jnp.exp(s - m_new)
        alpha = jnp.exp(m_old - m_new)
        l_ref[...] = alpha * l_ref[...] + jnp.sum(p, axis=-1, keepdims=True)
        acc_ref[...] = alpha * acc_ref[...] + _dot(p.astype(BF16), vj)
        m_ref[...] = m_new

    def body(j, carry):
        step(j, False)
        return carry

    lax.fori_loop(0, qi, body, 0)
    step(qi, True)

    o = acc_ref[...] / l_ref[...]
    o = o[:t] - _lam(lam_ref, l_init) * o[t:]
    o_ref[...] = (_rms(o, sub_ref[...]) * (1.0 - l_init)).astype(o_ref.dtype)


def _attn_prompt(lam, q, k, v, subln, layer, batch, seq, t):
    nq = seq // t
    kern = functools.partial(_attn_kernel, t=t, l_init=_l_init(layer))
    return pl.pallas_call(
        kern,
        grid=(batch, N_HEADS_A, nq),
        in_specs=[
            pl.BlockSpec((None, 4, DK), lambda b, h, i: (layer, 0, 0)),
            pl.BlockSpec((t, LANES), lambda b, h, i: (b * nq + i, h)),
            pl.BlockSpec((seq, LANES), lambda b, h, i: (b, h)),
            pl.BlockSpec((seq, LANES), lambda b, h, i: (b, h)),
            pl.BlockSpec((None, 1, DV), lambda b, h, i: (layer, 0, 0)),
        ],
        out_specs=pl.BlockSpec((t, LANES), lambda b, h, i: (b * nq + i, h)),
        out_shape=jax.ShapeDtypeStruct((batch * seq, N_HEADS_A * DV), BF16),
        scratch_shapes=[pltpu.VMEM((2 * t, 1), F32), pltpu.VMEM((2 * t, 1), F32),
                        pltpu.VMEM((2 * t, DV), F32)],
        compiler_params=pltpu.CompilerParams(
            dimension_semantics=("parallel", "parallel", "arbitrary"), vmem_limit_bytes=VMEM_LIMIT),
        name="attn_prompt",
    )(lam, q, k, v, subln)


def _dec_kernel(pt_ref, lam_ref, qm_ref, knew_ref, vnew_ref, sub_ref, *rest, g_pages, t_new, l_init):
    del pt_ref
    k_refs = rest[:g_pages]
    v_refs = rest[g_pages:2 * g_pages]
    o_ref, m_ref, l_ref, acc_ref, bias_ref = rest[2 * g_pages:]
    pg = pl.program_id(1)
    rows = N_HEADS_A * 2 * t_new

    def head_mask(n_cols):
        r = lax.broadcasted_iota(jnp.int32, (rows, n_cols), 0)
        c = lax.broadcasted_iota(jnp.int32, (rows, n_cols), 1)
        return r, c, (c % N_HEADS_A) == (r // (2 * t_new))

    @pl.when(pg == 0)
    def _():
        m_ref[...] = jnp.full_like(m_ref, NEG)
        l_ref[...] = jnp.zeros_like(l_ref)
        acc_ref[...] = jnp.zeros_like(acc_ref)
        _, _, same_head = head_mask(PAGE_SIZE * N_HEADS_A)
        bias_ref[...] = jnp.where(same_head, 0.0, NEG)

    qm = qm_ref[...]

    def update(s, v_rows):
        m_old = m_ref[...]
        m_new = jnp.maximum(m_old, jnp.max(s, axis=-1, keepdims=True))
        p = jnp.exp(s - m_new)
        alpha = jnp.exp(m_old - m_new)
        l_ref[...] = alpha * l_ref[...] + jnp.sum(p, axis=-1, keepdims=True)
        acc_ref[...] = alpha * acc_ref[...] + _dot(p.astype(BF16), v_rows)
        m_ref[...] = m_new

    for kr, vr in zip(k_refs, v_refs):
        k_rows = kr[...].reshape(PAGE_SIZE * N_HEADS_A, LANES).astype(BF16)
        v_rows = vr[...].reshape(PAGE_SIZE * N_HEADS_A, LANES).astype(BF16)
        update(_dot_nt(qm, k_rows) + bias_ref[...], v_rows)

    @pl.when(pg == pl.num_programs(1) - 1)
    def _():
        r, c, same_head = head_mask(t_new * N_HEADS_A)
        visible = same_head & ((c // N_HEADS_A) <= (r % t_new))
        update(jnp.where(visible, _dot_nt(qm, knew_ref[...]), NEG), vnew_ref[...])

        lam = _lam(lam_ref, l_init)
        o = acc_ref[...] / l_ref[...]
        sub = sub_ref[...]
        for h in range(N_HEADS_A):
            blk = o[h * 2 * t_new:(h + 1) * 2 * t_new]
            oh = blk[:t_new] - lam * blk[t_new:]
            o_ref[:, h * DV:(h + 1) * DV] = _rms(oh, sub) * (1.0 - l_init)


def _attn_sample(page_table, lam, qm, knew, vnew, subln, cache_k, cache_v, layer, g_pages, t_new):
    db, n_pages = page_table.shape
    width = N_HEADS_A * DV
    rows = N_HEADS_A * 2 * t_new
    kern = functools.partial(_dec_kernel, g_pages=g_pages, t_new=t_new, l_init=_l_init(layer))

    def page_spec(g):
        return pl.BlockSpec((None, None, PAGE_SIZE, N_HEADS_A, LANES),
                            lambda b, p, pt: (layer, pt[b, p * g_pages + g], 0, 0, 0))

    grid_spec = pltpu.PrefetchScalarGridSpec(
        num_scalar_prefetch=1,
        grid=(db, n_pages // g_pages),
        in_specs=[
            pl.BlockSpec((None, 4, DK), lambda b, p, pt: (layer, 0, 0)),
            pl.BlockSpec((None, rows, LANES), lambda b, p, pt: (b, 0, 0)),
            pl.BlockSpec((None, t_new * N_HEADS_A, LANES), lambda b, p, pt: (b, 0, 0)),
            pl.BlockSpec((None, t_new * N_HEADS_A, LANES), lambda b, p, pt: (b, 0, 0)),
            pl.BlockSpec((None, 1, DV), lambda b, p, pt: (layer, 0, 0)),
        ] + [page_spec(g) for g in range(g_pages)] + [page_spec(g) for g in range(g_pages)],
        out_specs=pl.BlockSpec((t_new, width), lambda b, p, pt: (b, 0)),
        scratch_shapes=[pltpu.VMEM((rows, 1), F32), pltpu.VMEM((rows, 1), F32),
                        pltpu.VMEM((rows, DV), F32),
                        pltpu.VMEM((rows, PAGE_SIZE * N_HEADS_A), F32)],
    )
    return pl.pallas_call(
        kern,
        grid_spec=grid_spec,
        out_shape=jax.ShapeDtypeStruct((db * t_new, width), F32),
        compiler_params=pltpu.CompilerParams(
            dimension_semantics=("parallel", "arbitrary"), vmem_limit_bytes=VMEM_LIMIT),
        name="attn_sample",
    )(page_table, lam, qm, knew, vnew, subln, *([cache_k] * g_pages), *([cache_v] * g_pages))


def _post_kernel(h_ref, a_ref, up_ref, halo_ref, ug_ref, vg_ref, pw_ref, ps_ref, ws_ref, gb_ref,
                 wo_ref, o_ref, ext_ref, cat_ref, *, tm, tiles_per_seq, d_attn, d_pool):
    i = pl.program_id(0)

    halo = halo_ref[...]
    if tiles_per_seq is not None:
        seq_tile = i % tiles_per_seq
        halo = jnp.where(seq_tile == 0, jnp.zeros_like(halo), halo)
        pos = seq_tile * tm + lax.broadcasted_iota(jnp.int32, (tm, LANES), 0)
    ext_ref[0:HALO, :] = halo
    ext_ref[HALO:HALO + tm, :] = up_ref[...]
    for g, w in enumerate(POOL_WINDOWS):
        sl = slice(g * LANES, (g + 1) * LANES)
        u = up_ref[:, sl]
        tot = u
        for k in range(1, w):
            tot = tot + ext_ref[HALO - k:HALO - k + tm, sl]
        if tiles_per_seq is not None:
            cnt = jnp.minimum(w, pos + 1).astype(F32)
            mean = tot / cnt
        else:
            mean = tot / float(w)
        y = _dot((mean - u).astype(BF16), pw_ref[g]) * ps_ref[:, sl]
        cat_ref[:, d_attn + g * LANES:d_attn + (g + 1) * LANES] = y.astype(BF16)

    r = lax.broadcasted_iota(jnp.int32, (CHUNK, CHUNK), 0)
    c = lax.broadcasted_iota(jnp.int32, (CHUNK, CHUNK), 1)
    o_g = d_attn + d_pool
    for hh in range(N_HEADS_C):
        wsc = jnp.where(r >= c, ws_ref[hh], 0.0).astype(BF16)
        bias = gb_ref[hh]
        cs = slice(hh * CHUNK, (hh + 1) * CHUNK)
        for n in range(tm // CHUNK):
            rs = slice(n * CHUNK, (n + 1) * CHUNK)
            gate = _dot(wsc, vg_ref[rs, cs].astype(BF16)) + bias
            cat_ref[rs, o_g + hh * CHUNK:o_g + (hh + 1) * CHUNK] = (ug_ref[rs, cs] * gate).astype(BF16)

    cat_ref[:, 0:d_attn] = a_ref[...]
    o_ref[...] = h_ref[...] + _dot(cat_ref[...], wo_ref[...])


def _post(h, a, up, halo_src, ug, vg, pool_w, pool_scale, ws, gb, w_out, layer, tm, tiles_per_seq):
    m, d = h.shape
    d_attn = a.shape[1]
    d_pool = up.shape[1]
    d_gmlp = ug.shape[1]
    row = lambda i: (i, 0)
    if tiles_per_seq is not None:
        halo_map = lambda i: (jnp.maximum(i * (tm // HALO) - 1, 0), 0)
    else:
        halo_map = row
    kern = functools.partial(_post_kernel, tm=tm, tiles_per_seq=tiles_per_seq, d_attn=d_attn,
                             d_pool=d_pool)
    return pl.pallas_call(
        kern,
        grid=(m // tm,),
        in_specs=[
            pl.BlockSpec((tm, d), row),
            pl.BlockSpec((tm, d_attn), row),
            pl.BlockSpec((tm, d_pool), row),
            pl.BlockSpec((HALO, d_pool), halo_map),
            pl.BlockSpec((tm, d_gmlp), row),
            pl.BlockSpec((tm, d_gmlp), row),
            pl.BlockSpec((None, len(POOL_WINDOWS), LANES, LANES), lambda i: (layer, 0, 0, 0)),
            pl.BlockSpec((None, 1, d_pool), lambda i: (layer, 0, 0)),
            pl.BlockSpec((None, N_HEADS_C, CHUNK, CHUNK), lambda i: (layer, 0, 0, 0)),
            pl.BlockSpec((None, N_HEADS_C, CHUNK, CHUNK), lambda i: (layer, 0, 0, 0)),
            pl.BlockSpec((None, d, d), lambda i: (layer, 0, 0)),
        ],
        out_specs=pl.BlockSpec((tm, d), row),
        out_shape=jax.ShapeDtypeStruct((m, d), F32),
        scratch_shapes=[pltpu.VMEM((HALO + tm, d_pool), F32), pltpu.VMEM((tm, d), BF16)],
        compiler_params=pltpu.CompilerParams(
            dimension_semantics=("parallel",), vmem_limit_bytes=VMEM_LIMIT),
        name="post",
    )(h, a, up, halo_src, ug, vg, pool_w, pool_scale, ws, gb, w_out)


def _rope_tables(pos):
    half = DK // 2
    inv_freq = jnp.power(ROPE_THETA, -jnp.arange(half, dtype=F32) / half)
    ang = pos.astype(F32)[:, None] * inv_freq[None, :]
    cos = jnp.tile(jnp.cos(ang), (1, LANES // half))
    sign = jnp.where((jnp.arange(LANES) % DK) < half, -1.0, 1.0).astype(F32)
    sin = jnp.tile(jnp.sin(ang), (1, LANES // half)) * sign[None, :]
    return cos, sin


def kernel(x_prompt, x_sample, cache_k, cache_v, state_pool, page_table, norm_ffn1, w1_gate, w1_up, w1_down, norm_mix, w_in, w_out, q_norm, k_norm, lambda_q1, lambda_k1, lambda_q2, lambda_k2, subln, pool_w, pool_scale, gmlp_norm, gmlp_ws, gmlp_b, norm_ffn2, w2_gate, w2_up, w2_down):
    b, seq, d = x_prompt.shape
    db, t_new, _ = x_sample.shape
    depth = w_in.shape[0]
    n_pages = page_table.shape[1]
    past = n_pages * PAGE_SIZE
    width = N_HEADS_A * DV
    d_pool = pool_scale.shape[1]
    d_gmlp = gmlp_norm.shape[1]
    ms = db * t_new

    tm_ffn, tf = 512, 512
    tm_proj = 256
    t_attn = 256
    tm_post = 256
    g_pages = 4

    bf = lambda w: w.astype(BF16)
    w1g, w1u, w1d = bf(w1_gate), bf(w1_up), bf(w1_down)
    w2g, w2u, w2d = bf(w2_gate), bf(w2_up), bf(w2_down)
    w_in_b, w_out_b, pool_w_b = bf(w_in), bf(w_out), bf(pool_w)
    row3 = lambda p: p.reshape(depth, 1, -1)
    n1, nm, n2 = row3(norm_ffn1), row3(norm_mix), row3(norm_ffn2)
    qg = row3(jnp.tile(q_norm, (1, LANES // DK)))
    kg = row3(jnp.tile(k_norm, (1, LANES // DK)))
    gg, sub, ps = row3(gmlp_norm), row3(subln), row3(pool_scale)
    lam = jnp.stack([lambda_q1, lambda_k1, lambda_q2, lambda_k2], axis=1)
    gb = jnp.broadcast_to(gmlp_b[:, :, :, None], gmlp_b.shape + (CHUNK,))
    grp = jnp.arange(LANES) // DK
    avg = jnp.where(grp[:, None] == grp[None, :], 1.0 / DK, 0.0).astype(BF16)

    cos_p, sin_p = _rope_tables(jnp.arange(seq, dtype=jnp.int32))
    cos_s, sin_s = _rope_tables(past + (jnp.arange(ms, dtype=jnp.int32) % t_new))


    hp = x_prompt.reshape(b * seq, d)
    hs = x_sample.reshape(ms, d)
    kp_l, vp_l, poolp_l, ks_l, vs_l, pools_l, gv_l = [], [], [], [], [], [], []
    pad_rows = lambda x: jnp.pad(x.reshape(db, t_new, -1), ((0, 0), (0, CHUNK - t_new), (0, 0))
                                 ).reshape(db * CHUNK, -1)

    for l in range(depth):
        hp = _ffn(hp, n1, w1g, w1u, w1d, l, tm_ffn, tf)
        hs = _ffn(hs, n1, w1g, w1u, w1d, l, ms, tf)

        q, k, kb, v, vb, up, ug, vg = _proj(hp, nm, w_in_b, qg, kg, gg, cos_p, sin_p, avg, l,
                                            tm_proj, seq // tm_proj)
        a = _attn_prompt(lam, q, kb, vb, sub, l, b, seq, t_attn)
        hp = _post(hp, a, up, up, ug, vg, pool_w_b, ps, gmlp_ws, gb, w_out_b, l, tm_post,
                   seq // tm_post)
        kp_l.append(k)
        vp_l.append(v)
        poolp_l.append(up.reshape(b, seq, d_pool)[:, seq - POOL_BUF:])

        q, k, kb, v, vb, us, ug, vg = _proj(hs, nm, w_in_b, qg, kg, gg, cos_s, sin_s, avg, l, ms, 1)
        q5 = q.reshape(db, t_new, N_HEADS_A, 2, DK).transpose(0, 2, 1, 3, 4)
        comp = jnp.arange(2)
        qm = jnp.where(comp[None, None, :, None, None, None] == comp[None, None, None, None, :, None],
                       q5[:, :, None], jnp.zeros((), BF16))
        qm = qm.reshape(db, N_HEADS_A * 2 * t_new, LANES)
        knew = kb.reshape(db, t_new * N_HEADS_A, LANES)
        vnew = vb.reshape(db, t_new * N_HEADS_A, LANES)
        a = _attn_sample(page_table, lam, qm, knew, vnew, sub, cache_k, cache_v, l, g_pages, t_new)
        ext = jnp.concatenate([state_pool[l], us.reshape(db, t_new, d_pool)], axis=1)
        halo = jnp.pad(state_pool[l], ((0, 0), (HALO - POOL_BUF, 0), (0, 0))).reshape(db * HALO, d_pool)
        hs_pad = _post(pad_rows(hs), pad_rows(a.astype(BF16)), pad_rows(us), halo, pad_rows(ug),
                       pad_rows(vg), pool_w_b, ps, gmlp_ws, gb, w_out_b, l, CHUNK, None)
        hs = hs_pad.reshape(db, CHUNK, d)[:, :t_new].reshape(ms, d)
        ks_l.append(k)
        vs_l.append(v)
        pools_l.append(ext[:, t_new:])
        gv_l.append(vg)

        hp = _ffn(hp, n2, w2g, w2u, w2d, l, tm_ffn, tf)
        hs = _ffn(hs, n2, w2g, w2u, w2d, l, ms, tf)

    n_pg = seq // PAGE_SIZE
    return (
        hp.reshape(b, seq, d),
        hs.reshape(db, t_new, d),
        jnp.stack(kp_l).reshape(depth, b, n_pg, PAGE_SIZE, N_HEADS_A, 2 * DK),
        jnp.stack(vp_l).reshape(depth, b, n_pg, PAGE_SIZE, N_HEADS_A, DV),
        jnp.stack(poolp_l),
        jnp.stack(ks_l).reshape(depth, db, t_new, N_HEADS_A, 2 * DK),
        jnp.stack(vs_l).reshape(depth, db, t_new, N_HEADS_A, DV),
        jnp.stack(pools_l),
        jnp.stack(gv_l).reshape(depth, db, t_new, d_gmlp),
    )
```

```python
import functools
import math

import jax
import jax.numpy as jnp
from jax import lax
from jax.experimental import pallas as pl
from jax.experimental.pallas import tpu as pltpu

F32 = jnp.float32
BF16 = jnp.bfloat16

EPS = 1e-6
ROPE_THETA = 10000.0
N_HEADS_A = 8
DV = 128
DK = 64
POOL_WINDOWS = (2, 4, 8, 16)
POOL_BUF = 15
HALO = 16
CHUNK = 128
N_HEADS_C = 4
PAGE_SIZE = 128
LANES = 128
NEG = -1e30
VMEM_LIMIT = 56 * 1024 * 1024

_NT = (((1,), (1,)), ((), ()))


def _dot(a, b):
    return jnp.dot(a, b, preferred_element_type=F32)


def _dot_nt(a, b):
    return lax.dot_general(a, b, _NT, preferred_element_type=F32)


def _rms(x, g):
    ms = jnp.mean(x * x, axis=-1, keepdims=True)
    return x * lax.rsqrt(ms + EPS) * g


def _gelu(x):
    return 0.5 * x * (1.0 + lax.erf(x * (2.0 ** -0.5)))


def _l_init(layer):
    return 0.8 - 0.6 * math.exp(-0.3 * layer)


def _lam(lam_ref, l_init):
    lp = lam_ref[...]
    s1 = jnp.sum(lp[0:1] * lp[1:2], axis=-1, keepdims=True)
    s2 = jnp.sum(lp[2:3] * lp[3:4], axis=-1, keepdims=True)
    return jnp.exp(s1) - jnp.exp(s2) + l_init


def _ffn_kernel(x_ref, g_ref, wg_ref, wu_ref, wd_ref, o_ref, xn_ref):
    j = pl.program_id(1)

    @pl.when(j == 0)
    def _():
        xn_ref[...] = _rms(x_ref[...], g_ref[...]).astype(BF16)
        o_ref[...] = jnp.zeros_like(o_ref)

    xn = xn_ref[...]
    gate = _dot(xn, wg_ref[...].astype(BF16))
    up = _dot(xn, wu_ref[...].astype(BF16))
    act = (gate * jax.nn.sigmoid(gate) * up).astype(BF16)
    o_ref[...] += _dot(act, wd_ref[...].astype(BF16))

    @pl.when(j == pl.num_programs(1) - 1)
    def _():
        o_ref[...] = x_ref[...] + 0.5 * o_ref[...]


def _ffn(x, g, wg, wu, wd, layer, tm, tf):
    m, d = x.shape
    f = wg.shape[-1]
    return pl.pallas_call(
        _ffn_kernel,
        grid=(m // tm, f // tf),
        in_specs=[
            pl.BlockSpec((tm, d), lambda i, j: (i, 0)),
            pl.BlockSpec((None, 1, d), lambda i, j: (layer, 0, 0)),
            pl.BlockSpec((None, d, tf), lambda i, j: (layer, 0, j)),
            pl.BlockSpec((None, d, tf), lambda i, j: (layer, 0, j)),
            pl.BlockSpec((None, tf, d), lambda i, j: (layer, j, 0)),
        ],
        out_specs=pl.BlockSpec((tm, d), lambda i, j: (i, 0)),
        out_shape=jax.ShapeDtypeStruct((m, d), F32),
        scratch_shapes=[pltpu.VMEM((tm, d), BF16)],
        compiler_params=pltpu.CompilerParams(
            dimension_semantics=("parallel", "arbitrary"), vmem_limit_bytes=VMEM_LIMIT),
        name="ffn",
    )(x, g, wg, wu, wd)


def _proj_kernel(x_ref, g_ref, w_ref, qg_ref, kg_ref, gg_ref, cos_ref, sin_ref, avg_ref,
                 q_ref, k_ref, kb_ref, v_ref, vb_ref, up_ref, ug_ref, vg_ref,
                 *, d_attn, d_pool, d_gmlp, attn_scale):
    xn = _rms(x_ref[...], g_ref[...]).astype(BF16)
    tm = xn.shape[0]
    cos = cos_ref[...]
    sin = sin_ref[...]
    avg = avg_ref[...]
    lane = lax.broadcasted_iota(jnp.int32, (tm, LANES), 1)
    first_half = (lane % DK) < (DK // 2)

    def norm_rope(z, gain):
        z2 = z * z
        hi = z2.astype(BF16)
        lo = (z2 - hi.astype(F32)).astype(BF16)
        ms = _dot(hi, avg) + _dot(lo, avg)
        y = z * lax.rsqrt(ms + EPS) * gain
        partner = jnp.where(first_half, pltpu.roll(y, LANES - DK // 2, 1), pltpu.roll(y, DK // 2, 1))
        return y * cos + partner * sin

    zq = _dot(xn, w_ref[:, 0:d_attn])
    qg = qg_ref[...]
    for c in range(d_attn // LANES):
        sl = slice(c * LANES, (c + 1) * LANES)
        q_ref[:, sl] = (norm_rope(zq[:, sl], qg) * attn_scale).astype(BF16)

    zk = _dot(xn, w_ref[:, d_attn:2 * d_attn])
    kg = kg_ref[...]
    for c in range(d_attn // LANES):
        sl = slice(c * LANES, (c + 1) * LANES)
        kr = norm_rope(zk[:, sl], kg)
        k_ref[:, sl] = kr
        kb_ref[:, sl] = kr.astype(BF16)

    zv = _dot(xn, w_ref[:, 2 * d_attn:3 * d_attn])
    v_ref[...] = zv
    vb_ref[...] = zv.astype(BF16)

    o = 3 * d_attn
    up_ref[...] = _dot(xn, w_ref[:, o:o + d_pool])
    o += d_pool
    ug_ref[...] = _gelu(_dot(xn, w_ref[:, o:o + d_gmlp]))
    o += d_gmlp
    zg = _gelu(_dot(xn, w_ref[:, o:o + d_gmlp]))
    vg_ref[...] = _rms(zg, gg_ref[...])


def _proj(x, g, w_in, qg, kg, gg, cos, sin, avg, layer, tm, table_tiles):
    m, d = x.shape
    d_in = w_in.shape[-1]
    d_attn = N_HEADS_A * DV
    d_pool = 4 * LANES
    d_gmlp = N_HEADS_C * CHUNK
    row = lambda i: (i, 0)
    tab = lambda i: (i % table_tiles, 0)
    par = lambda i: (layer, 0, 0)
    kern = functools.partial(_proj_kernel, d_attn=d_attn, d_pool=d_pool, d_gmlp=d_gmlp,
                             attn_scale=DK ** -0.5 * math.log2(math.e))
    return pl.pallas_call(
        kern,
        grid=(m // tm,),
        in_specs=[
            pl.BlockSpec((tm, d), row),
            pl.BlockSpec((None, 1, d), par),
            pl.BlockSpec((None, d, d_in), par),
            pl.BlockSpec((None, 1, LANES), par),
            pl.BlockSpec((None, 1, LANES), par),
            pl.BlockSpec((None, 1, d_gmlp), par),
            pl.BlockSpec((tm, LANES), tab),
            pl.BlockSpec((tm, LANES), tab),
            pl.BlockSpec((LANES, LANES), lambda i: (0, 0)),
        ],
        out_specs=[
            pl.BlockSpec((tm, d_attn), row),
            pl.BlockSpec((tm, d_attn), row),
            pl.BlockSpec((tm, d_attn), row),
            pl.BlockSpec((tm, d_attn), row),
            pl.BlockSpec((tm, d_attn), row),
            pl.BlockSpec((tm, d_pool), row),
            pl.BlockSpec((tm, d_gmlp), row),
            pl.BlockSpec((tm, d_gmlp), row),
        ],
        out_shape=[
            jax.ShapeDtypeStruct((m, d_attn), BF16),
            jax.ShapeDtypeStruct((m, d_attn), F32),
            jax.ShapeDtypeStruct((m, d_attn), BF16),
            jax.ShapeDtypeStruct((m, d_attn), F32),
            jax.ShapeDtypeStruct((m, d_attn), BF16),
            jax.ShapeDtypeStruct((m, d_pool), F32),
            jax.ShapeDtypeStruct((m, d_gmlp), F32),
            jax.ShapeDtypeStruct((m, d_gmlp), F32),
        ],
        compiler_params=pltpu.CompilerParams(
            dimension_semantics=("parallel",), vmem_limit_bytes=VMEM_LIMIT),
        name="proj",
    )(x, g, w_in, qg, kg, gg, cos, sin, avg)


def _attn_kernel(lam_ref, q_ref, k_ref, v_ref, sub_ref, o_ref, vt_ref, m_ref, l_ref, acc_ref,
                 *, t, l_init):
    qi = pl.program_id(2)

    @pl.when(qi == 0)
    def _():
        for jb in range(vt_ref.shape[0]):
            vt_ref[jb] = v_ref[jb * t:(jb + 1) * t, :].astype(F32).T.astype(BF16)

    q = q_ref[...]
    lane = lax.broadcasted_iota(jnp.int32, q.shape, 1)
    zero = jnp.zeros_like(q)
    q2 = jnp.concatenate([jnp.where(lane < DK, q, zero), jnp.where(lane >= DK, q, zero)], axis=0)

    m_ref[...] = jnp.full_like(m_ref, NEG)
    l_ref[...] = jnp.zeros_like(l_ref)
    acc_ref[...] = jnp.zeros_like(acc_ref)

    def step(j, masked):
        kj = k_ref[pl.ds(pl.multiple_of(j * t, t), t), :]
        s = _dot_nt(kj, q2)
        if masked:
            r = lax.broadcasted_iota(jnp.int32, s.shape, 0)
            c = lax.broadcasted_iota(jnp.int32, s.shape, 1)
            s = jnp.where(r <= jnp.where(c >= t, c - t, c), s, NEG)
        m_old = m_ref[...]
        m_new = jnp.maximum(m_old, jnp.max(s, axis=0, keepdims=True))
        p = jnp.exp2(s - m_new)
        alpha = jnp.exp2(m_old - m_new)
        l_ref[...] = alpha * l_ref[...] + jnp.sum(p, axis=0, keepdims=True)
        acc_ref[...] = alpha * acc_ref[...] + _dot(vt_ref[j], p.astype(BF16))
        m_ref[...] = m_new

    def body(j, carry):
        step(j, False)
        return carry

    lax.fori_loop(0, qi, body, 0)
    step(qi, True)

    o = acc_ref[...] / l_ref[...]
    o = o[:, :t] - _lam(lam_ref, l_init) * o[:, t:]
    ms = jnp.mean(o * o, axis=0, keepdims=True)
    y = (o * lax.rsqrt(ms + EPS)).T
    o_ref[...] = (y * sub_ref[...] * (1.0 - l_init)).astype(o_ref.dtype)


def _attn_prompt(lam, q, k, v, subln, layer, batch, seq, t):
    nq = seq // t
    kern = functools.partial(_attn_kernel, t=t, l_init=_l_init(layer))
    return pl.pallas_call(
        kern,
        grid=(batch, N_HEADS_A, nq),
        in_specs=[
            pl.BlockSpec((None, 4, DK), lambda b, h, i: (layer, 0, 0)),
            pl.BlockSpec((t, LANES), lambda b, h, i: (b * nq + i, h)),
            pl.BlockSpec((seq, LANES), lambda b, h, i: (b, h)),
            pl.BlockSpec((seq, LANES), lambda b, h, i: (b, h)),
            pl.BlockSpec((None, 1, DV), lambda b, h, i: (layer, 0, 0)),
        ],
        out_specs=pl.BlockSpec((t, LANES), lambda b, h, i: (b * nq + i, h)),
        out_shape=jax.ShapeDtypeStruct((batch * seq, N_HEADS_A * DV), BF16),
        scratch_shapes=[pltpu.VMEM((nq, DV, t), BF16),
                        pltpu.VMEM((1, 2 * t), F32), pltpu.VMEM((1, 2 * t), F32),
                        pltpu.VMEM((DV, 2 * t), F32)],
        compiler_params=pltpu.CompilerParams(
            dimension_semantics=("parallel", "parallel", "arbitrary"), vmem_limit_bytes=VMEM_LIMIT),
        name="attn_prompt",
    )(lam, q, k, v, subln)


def _dec_kernel(pt_ref, lam_ref, qm_ref, knew_ref, vnew_ref, sub_ref, *rest, g_pages, t_new, l_init):
    del pt_ref
    k_refs = rest[:g_pages]
    v_refs = rest[g_pages:2 * g_pages]
    o_ref, m_ref, l_ref, acc_ref, bias_ref = rest[2 * g_pages:]
    pg = pl.program_id(1)
    rows = N_HEADS_A * 2 * t_new

    def head_mask(n_cols):
        r = lax.broadcasted_iota(jnp.int32, (rows, n_cols), 0)
        c = lax.broadcasted_iota(jnp.int32, (rows, n_cols), 1)
        return r, c, (c % N_HEADS_A) == (r // (2 * t_new))

    @pl.when(pg == 0)
    def _():
        m_ref[...] = jnp.full_like(m_ref, NEG)
        l_ref[...] = jnp.zeros_like(l_ref)
        acc_ref[...] = jnp.zeros_like(acc_ref)
        _, _, same_head = head_mask(PAGE_SIZE * N_HEADS_A)
        bias_ref[...] = jnp.where(same_head, 0.0, NEG)

    qm = qm_ref[...]

    def update(ss, vs):
        m_old = m_ref[...]
        m_new = m_old
        for s in ss:
            m_new = jnp.maximum(m_new, jnp.max(s, axis=-1, keepdims=True))
        alpha = jnp.exp2(m_old - m_new)
        l_new = alpha * l_ref[...]
        acc = alpha * acc_ref[...]
        for s, v_rows in zip(ss, vs):
            p = jnp.exp2(s - m_new)
            l_new = l_new + jnp.sum(p, axis=-1, keepdims=True)
            acc = acc + _dot(p.astype(BF16), v_rows)
        l_ref[...] = l_new
        acc_ref[...] = acc
        m_ref[...] = m_new

    def page_rows(ref):
        return ref[...].reshape(PAGE_SIZE * N_HEADS_A, LANES).astype(BF16)

    bias = bias_ref[...]
    update([_dot_nt(qm, page_rows(kr)) + bias for kr in k_refs], [page_rows(vr) for vr in v_refs])

    @pl.when(pg == pl.num_programs(1) - 1)
    def _():
        r, c, same_head = head_mask(t_new * N_HEADS_A)
        visible = same_head & ((c // N_HEADS_A) <= (r % t_new))
        update([jnp.where(visible, _dot_nt(qm, knew_ref[...]), NEG)], [vnew_ref[...]])

        lam = _lam(lam_ref, l_init)
        o = acc_ref[...] / l_ref[...]
        sub = sub_ref[...]
        for h in range(N_HEADS_A):
            blk = o[h * 2 * t_new:(h + 1) * 2 * t_new]
            oh = blk[:t_new] - lam * blk[t_new:]
            o_ref[:, h * DV:(h + 1) * DV] = _rms(oh, sub) * (1.0 - l_init)


def _attn_sample(page_table, lam, qm, knew, vnew, subln, cache_k, cache_v, layer, g_pages, t_new):
    db, n_pages = page_table.shape
    width = N_HEADS_A * DV
    rows = N_HEADS_A * 2 * t_new
    kern = functools.partial(_dec_kernel, g_pages=g_pages, t_new=t_new, l_init=_l_init(layer))

    def page_spec(g):
        return pl.BlockSpec((None, None, PAGE_SIZE, N_HEADS_A, LANES),
                            lambda b, p, pt: (layer, pt[b, p * g_pages + g], 0, 0, 0))

    grid_spec = pltpu.PrefetchScalarGridSpec(
        num_scalar_prefetch=1,
        grid=(db, n_pages // g_pages),
        in_specs=[
            pl.BlockSpec((None, 4, DK), lambda b, p, pt: (layer, 0, 0)),
            pl.BlockSpec((None, rows, LANES), lambda b, p, pt: (b, 0, 0)),
            pl.BlockSpec((None, t_new * N_HEADS_A, LANES), lambda b, p, pt: (b, 0, 0)),
            pl.BlockSpec((None, t_new * N_HEADS_A, LANES), lambda b, p, pt: (b, 0, 0)),
            pl.BlockSpec((None, 1, DV), lambda b, p, pt: (layer, 0, 0)),
        ] + [page_spec(g) for g in range(g_pages)] + [page_spec(g) for g in range(g_pages)],
        out_specs=pl.BlockSpec((t_new, width), lambda b, p, pt: (b, 0)),
        scratch_shapes=[pltpu.VMEM((rows, 1), F32), pltpu.VMEM((rows, 1), F32),
                        pltpu.VMEM((rows, DV), F32),
                        pltpu.VMEM((rows, PAGE_SIZE * N_HEADS_A), F32)],
    )
    return pl.pallas_call(
        kern,
        grid_spec=grid_spec,
        out_shape=jax.ShapeDtypeStruct((db * t_new, width), F32),
        compiler_params=pltpu.CompilerParams(
            dimension_semantics=("parallel", "arbitrary"), vmem_limit_bytes=VMEM_LIMIT),
        name="attn_sample",
    )(page_table, lam, qm, knew, vnew, subln, *([cache_k] * g_pages), *([cache_v] * g_pages))


def _post_kernel(h_ref, a_ref, up_ref, halo_ref, ug_ref, vg_ref, pw_ref, ps_ref, ws_ref, gb_ref,
                 wo_ref, o_ref, ext_ref, cat_ref, *, tm, tiles_per_seq, d_attn, d_pool):
    i = pl.program_id(0)

    halo = halo_ref[...]
    if tiles_per_seq is not None:
        seq_tile = i % tiles_per_seq
        halo = jnp.where(seq_tile == 0, jnp.zeros_like(halo), halo)
        pos = seq_tile * tm + lax.broadcasted_iota(jnp.int32, (tm, LANES), 0)
    ext_ref[0:HALO, :] = halo
    ext_ref[HALO:HALO + tm, :] = up_ref[...]
    for g, w in enumerate(POOL_WINDOWS):
        sl = slice(g * LANES, (g + 1) * LANES)
        u = up_ref[:, sl]
        tot = u
        for k in range(1, w):
            tot = tot + ext_ref[HALO - k:HALO - k + tm, sl]
        if tiles_per_seq is not None:
            cnt = jnp.minimum(w, pos + 1).astype(F32)
            mean = tot / cnt
        else:
            mean = tot / float(w)
        y = _dot((mean - u).astype(BF16), pw_ref[g]) * ps_ref[:, sl]
        cat_ref[:, d_attn + g * LANES:d_attn + (g + 1) * LANES] = y.astype(BF16)

    r = lax.broadcasted_iota(jnp.int32, (CHUNK, CHUNK), 0)
    c = lax.broadcasted_iota(jnp.int32, (CHUNK, CHUNK), 1)
    o_g = d_attn + d_pool
    for hh in range(N_HEADS_C):
        wsc = jnp.where(r >= c, ws_ref[hh], 0.0).astype(BF16)
        bias = gb_ref[hh]
        cs = slice(hh * CHUNK, (hh + 1) * CHUNK)
        for n in range(tm // CHUNK):
            rs = slice(n * CHUNK, (n + 1) * CHUNK)
            gate = _dot(wsc, vg_ref[rs, cs].astype(BF16)) + bias
            cat_ref[rs, o_g + hh * CHUNK:o_g + (hh + 1) * CHUNK] = (ug_ref[rs, cs] * gate).astype(BF16)

    cat_ref[:, 0:d_attn] = a_ref[...]
    o_ref[...] = h_ref[...] + _dot(cat_ref[...], wo_ref[...])


def _post(h, a, up, halo_src, ug, vg, pool_w, pool_scale, ws, gb, w_out, layer, tm, tiles_per_seq):
    m, d = h.shape
    d_attn = a.shape[1]
    d_pool = up.shape[1]
    d_gmlp = ug.shape[1]
    row = lambda i: (i, 0)
    if tiles_per_seq is not None:
        halo_map = lambda i: (jnp.maximum(i * (tm // HALO) - 1, 0), 0)
    else:
        halo_map = row
    kern = functools.partial(_post_kernel, tm=tm, tiles_per_seq=tiles_per_seq, d_attn=d_attn,
                             d_pool=d_pool)
    return pl.pallas_call(
        kern,
        grid=(m // tm,),
        in_specs=[
            pl.BlockSpec((tm, d), row),
            pl.BlockSpec((tm, d_attn), row),
            pl.BlockSpec((tm, d_pool), row),
            pl.BlockSpec((HALO, d_pool), halo_map),
            pl.BlockSpec((tm, d_gmlp), row),
            pl.BlockSpec((tm, d_gmlp), row),
            pl.BlockSpec((None, len(POOL_WINDOWS), LANES, LANES), lambda i: (layer, 0, 0, 0)),
            pl.BlockSpec((None, 1, d_pool), lambda i: (layer, 0, 0)),
            pl.BlockSpec((None, N_HEADS_C, CHUNK, CHUNK), lambda i: (layer, 0, 0, 0)),
            pl.BlockSpec((None, N_HEADS_C, CHUNK, CHUNK), lambda i: (layer, 0, 0, 0)),
            pl.BlockSpec((None, d, d), lambda i: (layer, 0, 0)),
        ],
        out_specs=pl.BlockSpec((tm, d), row),
        out_shape=jax.ShapeDtypeStruct((m, d), F32),
        scratch_shapes=[pltpu.VMEM((HALO + tm, d_pool), F32), pltpu.VMEM((tm, d), BF16)],
        compiler_params=pltpu.CompilerParams(
            dimension_semantics=("parallel",), vmem_limit_bytes=VMEM_LIMIT),
        name="post",
    )(h, a, up, halo_src, ug, vg, pool_w, pool_scale, ws, gb, w_out)


def _rope_tables(pos):
    half = DK // 2
    inv_freq = jnp.power(ROPE_THETA, -jnp.arange(half, dtype=F32) / half)
    ang = pos.astype(F32)[:, None] * inv_freq[None, :]
    cos = jnp.tile(jnp.cos(ang), (1, LANES // half))
    sign = jnp.where((jnp.arange(LANES) % DK) < half, -1.0, 1.0).astype(F32)
    sin = jnp.tile(jnp.sin(ang), (1, LANES // half)) * sign[None, :]
    return cos, sin


def kernel(x_prompt, x_sample, cache_k, cache_v, state_pool, page_table, norm_ffn1, w1_gate, w1_up, w1_down, norm_mix, w_in, w_out, q_norm, k_norm, lambda_q1, lambda_k1, lambda_q2, lambda_k2, subln, pool_w, pool_scale, gmlp_norm, gmlp_ws, gmlp_b, norm_ffn2, w2_gate, w2_up, w2_down):
    b, seq, d = x_prompt.shape
    db, t_new, _ = x_sample.shape
    depth = w_in.shape[0]
    n_pages = page_table.shape[1]
    past = n_pages * PAGE_SIZE
    width = N_HEADS_A * DV
    d_pool = pool_scale.shape[1]
    d_gmlp = gmlp_norm.shape[1]
    ms = db * t_new

    tm_ffn, tf = 1024, 256
    tm_proj = 256
    t_attn = 512
    tm_post = 256
    g_pages = 4

    bf = lambda w: w.astype(BF16)
    w1g, w1u, w1d = w1_gate, w1_up, w1_down
    w2g, w2u, w2d = w2_gate, w2_up, w2_down
    w_in_b, w_out_b, pool_w_b = bf(w_in), bf(w_out), bf(pool_w)
    row3 = lambda p: p.reshape(depth, 1, -1)
    n1, nm, n2 = row3(norm_ffn1), row3(norm_mix), row3(norm_ffn2)
    qg = row3(jnp.tile(q_norm, (1, LANES // DK)))
    kg = row3(jnp.tile(k_norm, (1, LANES // DK)))
    gg, sub, ps = row3(gmlp_norm), row3(subln), row3(pool_scale)
    lam = jnp.stack([lambda_q1, lambda_k1, lambda_q2, lambda_k2], axis=1)
    gb = jnp.broadcast_to(gmlp_b[:, :, :, None], gmlp_b.shape + (CHUNK,))
    grp = jnp.arange(LANES) // DK
    avg = jnp.where(grp[:, None] == grp[None, :], 1.0 / DK, 0.0).astype(BF16)

    cos_p, sin_p = _rope_tables(jnp.arange(seq, dtype=jnp.int32))
    cos_s, sin_s = _rope_tables(past + (jnp.arange(ms, dtype=jnp.int32) % t_new))


    hp = x_prompt.reshape(b * seq, d)
    hs = x_sample.reshape(ms, d)
    kp_l, vp_l, poolp_l, ks_l, vs_l, pools_l, gv_l = [], [], [], [], [], [], []
    pad_rows = lambda x: jnp.pad(x.reshape(db, t_new, -1), ((0, 0), (0, CHUNK - t_new), (0, 0))
                                 ).reshape(db * CHUNK, -1)

    for l in range(depth):
        hp = _ffn(hp, n1, w1g, w1u, w1d, l, tm_ffn, tf)
        hs = _ffn(hs, n1, w1g, w1u, w1d, l, ms, tf)

        q, k, kb, v, vb, up, ug, vg = _proj(hp, nm, w_in_b, qg, kg, gg, cos_p, sin_p, avg, l,
                                            tm_proj, seq // tm_proj)
        a = _attn_prompt(lam, q, kb, vb, sub, l, b, seq, t_attn)
        hp = _post(hp, a, up, up, ug, vg, pool_w_b, ps, gmlp_ws, gb, w_out_b, l, tm_post,
                   seq // tm_post)
        kp_l.append(k)
        vp_l.append(v)
        poolp_l.append(up.reshape(b, seq, d_pool)[:, seq - POOL_BUF:])

        q, k, kb, v, vb, us, ug, vg = _proj(hs, nm, w_in_b, qg, kg, gg, cos_s, sin_s, avg, l, ms, 1)
        q5 = q.reshape(db, t_new, N_HEADS_A, 2, DK).transpose(0, 2, 1, 3, 4)
        comp = jnp.arange(2)
        qm = jnp.where(comp[None, None, :, None, None, None] == comp[None, None, None, None, :, None],
                       q5[:, :, None], jnp.zeros((), BF16))
        qm = qm.reshape(db, N_HEADS_A * 2 * t_new, LANES)
        knew = kb.reshape(db, t_new * N_HEADS_A, LANES)
        vnew = vb.reshape(db, t_new * N_HEADS_A, LANES)
        a = _attn_sample(page_table, lam, qm, knew, vnew, sub, cache_k, cache_v, l, g_pages, t_new)
        ext = jnp.concatenate([state_pool[l], us.reshape(db, t_new, d_pool)], axis=1)
        halo = jnp.pad(state_pool[l], ((0, 0), (HALO - POOL_BUF, 0), (0, 0))).reshape(db * HALO, d_pool)
        hs_pad = _post(pad_rows(hs), pad_rows(a.astype(BF16)), pad_rows(us), halo, pad_rows(ug),
                       pad_rows(vg), pool_w_b, ps, gmlp_ws, gb, w_out_b, l, CHUNK, None)
        hs = hs_pad.reshape(db, CHUNK, d)[:, :t_new].reshape(ms, d)
        ks_l.append(k)
        vs_l.append(v)
        pools_l.append(ext[:, t_new:])
        gv_l.append(vg)

        hp = _ffn(hp, n2, w2g, w2u, w2d, l, tm_ffn, tf)
        hs = _ffn(hs, n2, w2g, w2u, w2d, l, ms, tf)

    n_pg = seq // PAGE_SIZE
    return (
        hp.reshape(b, seq, d),
        hs.reshape(db, t_new, d),
        jnp.stack(kp_l).reshape(depth, b, n_pg, PAGE_SIZE, N_HEADS_A, 2 * DK),
        jnp.stack(vp_l).reshape(depth, b, n_pg, PAGE_SIZE, N_HEADS_A, DV),
        jnp.stack(poolp_l),
        jnp.stack(ks_l).reshape(depth, db, t_new, N_HEADS_A, 2 * DK),
        jnp.stack(vs_l).reshape(depth, db, t_new, N_HEADS_A, DV),
        jnp.stack(pools_l),
        jnp.stack(gv_l).reshape(depth, db, t_new, d_gmlp),
    )
```

```python
import functools
import math

import jax
import jax.numpy as jnp
from jax import lax
from jax.experimental import pallas as pl
from jax.experimental.pallas import tpu as pltpu

F32 = jnp.float32
BF16 = jnp.bfloat16

EPS = 1e-6
ROPE_THETA = 10000.0
N_HEADS_A = 8
DV = 128
DK = 64
POOL_WINDOWS = (2, 4, 8, 16)
POOL_BUF = 15
HALO = 16
CHUNK = 128
N_HEADS_C = 4
PAGE_SIZE = 128
LANES = 128
NEG = -1e30
VMEM_LIMIT = 56 * 1024 * 1024

_NT = (((1,), (1,)), ((), ()))


def _dot(a, b):
    return jnp.dot(a, b, preferred_element_type=F32)


def _dot_nt(a, b):
    return lax.dot_general(a, b, _NT, preferred_element_type=F32)


def _rms(x, g):
    ms = jnp.mean(x * x, axis=-1, keepdims=True)
    return x * lax.rsqrt(ms + EPS) * g


def _gelu(x):
    return 0.5 * x * (1.0 + lax.erf(x * (2.0 ** -0.5)))


def _l_init(layer):
    return 0.8 - 0.6 * math.exp(-0.3 * layer)


def _lam(lam_ref, l_init):
    lp = lam_ref[...]
    s1 = jnp.sum(lp[0:1] * lp[1:2], axis=-1, keepdims=True)
    s2 = jnp.sum(lp[2:3] * lp[3:4], axis=-1, keepdims=True)
    return jnp.exp(s1) - jnp.exp(s2) + l_init


def _ffn_kernel(x_ref, xs_ref, g_ref, wg_ref, wu_ref, wd_ref, o_ref, os_ref, xn_ref, *, tm):
    i = pl.program_id(0)
    j = pl.program_id(1)
    last_j = pl.num_programs(1) - 1

    def hidden(xn):
        gate = _dot(xn, wg_ref[...].astype(BF16))
        up = _dot(xn, wu_ref[...].astype(BF16))
        return (gate * jax.nn.sigmoid(gate) * up).astype(BF16)

    @pl.when(j == 0)
    def _():
        xn_ref[0:tm, :] = _rms(x_ref[...], g_ref[...]).astype(BF16)
        o_ref[...] = jnp.zeros_like(o_ref)

    @pl.when((j == 0) & (i == 0))
    def _():
        xn_ref[tm:, :] = _rms(xs_ref[...], g_ref[...]).astype(BF16)
        os_ref[...] = jnp.zeros_like(os_ref)

    @pl.when(i == 0)
    def _():
        act = hidden(xn_ref[...])
        wd = wd_ref[...].astype(BF16)
        o_ref[...] += _dot(act[:tm], wd)
        os_ref[...] += _dot(act[tm:], wd)

    @pl.when(i > 0)
    def _():
        o_ref[...] += _dot(hidden(xn_ref[0:tm, :]), wd_ref[...].astype(BF16))

    @pl.when(j == last_j)
    def _():
        o_ref[...] = x_ref[...] + 0.5 * o_ref[...]

    @pl.when((j == last_j) & (i == 0))
    def _():
        os_ref[...] = xs_ref[...] + 0.5 * os_ref[...]


def _ffn(x, xs, g, wg, wu, wd, layer, tm, tf):
    m, d = x.shape
    ms = xs.shape[0]
    f = wg.shape[-1]
    return pl.pallas_call(
        functools.partial(_ffn_kernel, tm=tm),
        grid=(m // tm, f // tf),
        in_specs=[
            pl.BlockSpec((tm, d), lambda i, j: (i, 0)),
            pl.BlockSpec((ms, d), lambda i, j: (0, 0)),
            pl.BlockSpec((None, 1, d), lambda i, j: (layer, 0, 0)),
            pl.BlockSpec((None, d, tf), lambda i, j: (layer, 0, j)),
            pl.BlockSpec((None, d, tf), lambda i, j: (layer, 0, j)),
            pl.BlockSpec((None, tf, d), lambda i, j: (layer, j, 0)),
        ],
        out_specs=[pl.BlockSpec((tm, d), lambda i, j: (i, 0)),
                   pl.BlockSpec((ms, d), lambda i, j: (0, 0))],
        out_shape=[jax.ShapeDtypeStruct((m, d), F32), jax.ShapeDtypeStruct((ms, d), F32)],
        scratch_shapes=[pltpu.VMEM((tm + ms, d), BF16)],
        compiler_params=pltpu.CompilerParams(
            dimension_semantics=("arbitrary", "arbitrary"), vmem_limit_bytes=VMEM_LIMIT),
        name="ffn",
    )(x, xs, g, wg, wu, wd)


def _proj_kernel(x_ref, g_ref, w_ref, qg_ref, kg_ref, gg_ref, cos_ref, sin_ref, avg_ref,
                 q_ref, k_ref, kb_ref, v_ref, vb_ref, up_ref, ug_ref, vg_ref,
                 *, d_attn, d_pool, d_gmlp, attn_scale):
    xn = _rms(x_ref[...], g_ref[...]).astype(BF16)
    tm = xn.shape[0]
    cos = cos_ref[...]
    sin = sin_ref[...]
    avg = avg_ref[...]
    lane = lax.broadcasted_iota(jnp.int32, (tm, LANES), 1)
    first_half = (lane % DK) < (DK // 2)

    def norm_rope(z, gain):
        z2 = z * z
        hi = z2.astype(BF16)
        lo = (z2 - hi.astype(F32)).astype(BF16)
        ms = _dot(hi, avg) + _dot(lo, avg)
        y = z * lax.rsqrt(ms + EPS) * gain
        partner = jnp.where(first_half, pltpu.roll(y, LANES - DK // 2, 1), pltpu.roll(y, DK // 2, 1))
        return y * cos + partner * sin

    zq = _dot(xn, w_ref[:, 0:d_attn])
    qg = qg_ref[...]
    for c in range(d_attn // LANES):
        sl = slice(c * LANES, (c + 1) * LANES)
        q_ref[:, sl] = (norm_rope(zq[:, sl], qg) * attn_scale).astype(BF16)

    zk = _dot(xn, w_ref[:, d_attn:2 * d_attn])
    kg = kg_ref[...]
    for c in range(d_attn // LANES):
        sl = slice(c * LANES, (c + 1) * LANES)
        kr = norm_rope(zk[:, sl], kg)
        k_ref[:, sl] = kr
        kb_ref[:, sl] = kr.astype(BF16)

    zv = _dot(xn, w_ref[:, 2 * d_attn:3 * d_attn])
    v_ref[...] = zv
    vb_ref[...] = zv.astype(BF16)

    o = 3 * d_attn
    up_ref[...] = _dot(xn, w_ref[:, o:o + d_pool])
    o += d_pool
    ug_ref[...] = _gelu(_dot(xn, w_ref[:, o:o + d_gmlp]))
    o += d_gmlp
    zg = _gelu(_dot(xn, w_ref[:, o:o + d_gmlp]))
    vg_ref[...] = _rms(zg, gg_ref[...])


def _proj(x, g, w_in, qg, kg, gg, cos, sin, avg, layer, tm, table_tiles):
    m, d = x.shape
    d_in = w_in.shape[-1]
    d_attn = N_HEADS_A * DV
    d_pool = 4 * LANES
    d_gmlp = N_HEADS_C * CHUNK
    row = lambda i: (i, 0)
    tab = lambda i: (i % table_tiles, 0)
    par = lambda i: (layer, 0, 0)
    kern = functools.partial(_proj_kernel, d_attn=d_attn, d_pool=d_pool, d_gmlp=d_gmlp,
                             attn_scale=DK ** -0.5 * math.log2(math.e))
    return pl.pallas_call(
        kern,
        grid=(m // tm,),
        in_specs=[
            pl.BlockSpec((tm, d), row),
            pl.BlockSpec((None, 1, d), par),
            pl.BlockSpec((None, d, d_in), par),
            pl.BlockSpec((None, 1, LANES), par),
            pl.BlockSpec((None, 1, LANES), par),
            pl.BlockSpec((None, 1, d_gmlp), par),
            pl.BlockSpec((tm, LANES), tab),
            pl.BlockSpec((tm, LANES), tab),
            pl.BlockSpec((LANES, LANES), lambda i: (0, 0)),
        ],
        out_specs=[
            pl.BlockSpec((tm, d_attn), row),
            pl.BlockSpec((tm, d_attn), row),
            pl.BlockSpec((tm, d_attn), row),
            pl.BlockSpec((tm, d_attn), row),
            pl.BlockSpec((tm, d_attn), row),
            pl.BlockSpec((tm, d_pool), row),
            pl.BlockSpec((tm, d_gmlp), row),
            pl.BlockSpec((tm, d_gmlp), row),
        ],
        out_shape=[
            jax.ShapeDtypeStruct((m, d_attn), BF16),
            jax.ShapeDtypeStruct((m, d_attn), F32),
            jax.ShapeDtypeStruct((m, d_attn), BF16),
            jax.ShapeDtypeStruct((m, d_attn), F32),
            jax.ShapeDtypeStruct((m, d_attn), BF16),
            jax.ShapeDtypeStruct((m, d_pool), F32),
            jax.ShapeDtypeStruct((m, d_gmlp), F32),
            jax.ShapeDtypeStruct((m, d_gmlp), F32),
        ],
        compiler_params=pltpu.CompilerParams(
            dimension_semantics=("parallel",), vmem_limit_bytes=VMEM_LIMIT),
        name="proj",
    )(x, g, w_in, qg, kg, gg, cos, sin, avg)


def _attn_kernel(lam_ref, q_ref, k_ref, v_ref, sub_ref, o_ref, vt_ref, m_ref, l_ref, acc_ref,
                 sa_ref, sb_ref, *, t, l_init):
    qi = pl.program_id(2)

    @pl.when(qi == 0)
    def _():
        for jb in range(vt_ref.shape[0]):
            vt_ref[jb] = v_ref[jb * t:(jb + 1) * t, :].astype(F32).T.astype(BF16)

    q = q_ref[...]
    lane = lax.broadcasted_iota(jnp.int32, q.shape, 1)
    zero = jnp.zeros_like(q)
    q2 = jnp.concatenate([jnp.where(lane < DK, q, zero), jnp.where(lane >= DK, q, zero)], axis=0)

    m_ref[...] = jnp.full_like(m_ref, NEG)
    l_ref[...] = jnp.zeros_like(l_ref)
    acc_ref[...] = jnp.zeros_like(acc_ref)

    def scores(j):
        kj = k_ref[pl.ds(pl.multiple_of(j * t, t), t), :]
        return _dot_nt(kj, q2)

    def update(j, s, masked):
        if masked:
            r = lax.broadcasted_iota(jnp.int32, s.shape, 0)
            c = lax.broadcasted_iota(jnp.int32, s.shape, 1)
            s = jnp.where(r <= jnp.where(c >= t, c - t, c), s, NEG)
        m_old = m_ref[...]
        m_new = jnp.maximum(m_old, jnp.max(s, axis=0, keepdims=True))
        p = jnp.exp2(s - m_new)
        alpha = jnp.exp2(m_old - m_new)
        l_ref[...] = alpha * l_ref[...] + jnp.sum(p, axis=0, keepdims=True)
        acc_ref[...] = alpha * acc_ref[...] + _dot(vt_ref[j], p.astype(BF16))
        m_ref[...] = m_new

    def body(i2, carry):
        j = 2 * i2
        sb_ref[...] = scores(j + 1)
        update(j, sa_ref[...], False)
        sa_ref[...] = scores(j + 2)
        update(j + 1, sb_ref[...], False)
        return carry

    sa_ref[...] = scores(0)
    lax.fori_loop(0, qi // 2, body, 0)

    @pl.when(qi % 2 == 0)
    def _():
        update(qi, sa_ref[...], True)

    @pl.when(qi % 2 == 1)
    def _():
        sb_ref[...] = scores(qi)
        update(qi - 1, sa_ref[...], False)
        update(qi, sb_ref[...], True)

    o = acc_ref[...] / l_ref[...]
    o = o[:, :t] - _lam(lam_ref, l_init) * o[:, t:]
    ms = jnp.mean(o * o, axis=0, keepdims=True)
    y = (o * lax.rsqrt(ms + EPS)).T
    o_ref[...] = (y * sub_ref[...] * (1.0 - l_init)).astype(o_ref.dtype)


def _attn_prompt(lam, q, k, v, subln, layer, batch, seq, t):
    nq = seq // t
    kern = functools.partial(_attn_kernel, t=t, l_init=_l_init(layer))
    return pl.pallas_call(
        kern,
        grid=(batch, N_HEADS_A, nq),
        in_specs=[
            pl.BlockSpec((None, 4, DK), lambda b, h, i: (layer, 0, 0)),
            pl.BlockSpec((t, LANES), lambda b, h, i: (b * nq + i, h)),
            pl.BlockSpec((seq, LANES), lambda b, h, i: (b, h)),
            pl.BlockSpec((seq, LANES), lambda b, h, i: (b, h)),
            pl.BlockSpec((None, 1, DV), lambda b, h, i: (layer, 0, 0)),
        ],
        out_specs=pl.BlockSpec((t, LANES), lambda b, h, i: (b * nq + i, h)),
        out_shape=jax.ShapeDtypeStruct((batch * seq, N_HEADS_A * DV), BF16),
        scratch_shapes=[pltpu.VMEM((nq, DV, t), BF16),
                        pltpu.VMEM((1, 2 * t), F32), pltpu.VMEM((1, 2 * t), F32),
                        pltpu.VMEM((DV, 2 * t), F32),
                        pltpu.VMEM((t, 2 * t), F32), pltpu.VMEM((t, 2 * t), F32)],
        compiler_params=pltpu.CompilerParams(
            dimension_semantics=("parallel", "parallel", "arbitrary"), vmem_limit_bytes=VMEM_LIMIT),
        name="attn_prompt",
    )(lam, q, k, v, subln)


def _dec_kernel(pt_ref, lam_ref, qm_ref, knew_ref, vnew_ref, sub_ref, *rest, g_pages, t_new, l_init):
    del pt_ref
    k_refs = rest[:g_pages]
    v_refs = rest[g_pages:2 * g_pages]
    o_ref, m_ref, l_ref, acc_ref = rest[2 * g_pages:]
    pg = pl.program_id(1)

    @pl.when(pg == 0)
    def _():
        m_ref[...] = jnp.full_like(m_ref, NEG)
        l_ref[...] = jnp.zeros_like(l_ref)
        acc_ref[...] = jnp.zeros_like(acc_ref)

    qm = qm_ref[...]

    def by_head(refs):
        return jnp.stack([
            jnp.concatenate([r[pl.ds(h, PAGE_SIZE, stride=N_HEADS_A), :] for r in refs], axis=0)
            for h in range(N_HEADS_A)]).astype(BF16)

    def update(s, v3):
        m_old = m_ref[...]
        m_new = jnp.maximum(m_old, jnp.max(s, axis=-1, keepdims=True))
        p = jnp.exp2(s - m_new)
        alpha = jnp.exp2(m_old - m_new)
        l_ref[...] = alpha * l_ref[...] + jnp.sum(p, axis=-1, keepdims=True)
        pv = jnp.einsum('hqk,hkd->hqd', p.astype(BF16), v3, preferred_element_type=F32)
        acc_ref[...] = alpha * acc_ref[...] + pv
        m_ref[...] = m_new

    def scores(k3):
        return jnp.einsum('hqd,hkd->hqk', qm, k3, preferred_element_type=F32)

    update(scores(by_head(k_refs)), by_head(v_refs))

    @pl.when(pg == pl.num_programs(1) - 1)
    def _():
        s_new = scores(knew_ref[...])
        r = lax.broadcasted_iota(jnp.int32, s_new.shape, 1)
        c = lax.broadcasted_iota(jnp.int32, s_new.shape, 2)
        update(jnp.where(c <= (r % t_new), s_new, NEG), vnew_ref[...])

        o = acc_ref[...] / l_ref[...]
        o = o[:, :t_new] - _lam(lam_ref, l_init) * o[:, t_new:]
        y = _rms(o, sub_ref[...]) * (1.0 - l_init)
        for h in range(N_HEADS_A):
            o_ref[:, h * DV:(h + 1) * DV] = y[h]


def _attn_sample(page_table, lam, qm, knew, vnew, subln, cache_k, cache_v, layer, g_pages, t_new):
    db, n_pages = page_table.shape
    width = N_HEADS_A * DV
    page_rows = PAGE_SIZE * N_HEADS_A
    kern = functools.partial(_dec_kernel, g_pages=g_pages, t_new=t_new, l_init=_l_init(layer))

    def page_spec(g):
        return pl.BlockSpec((None, None, page_rows, LANES),
                            lambda b, p, pt: (layer, pt[b, p * g_pages + g], 0, 0))

    kv_specs = [page_spec(g) for g in range(g_pages)]

    per_batch = lambda b, p, pt: (b, 0, 0, 0)
    grid_spec = pltpu.PrefetchScalarGridSpec(
        num_scalar_prefetch=1,
        grid=(db, n_pages // g_pages),
        in_specs=[
            pl.BlockSpec((None, 4, DK), lambda b, p, pt: (layer, 0, 0)),
            pl.BlockSpec((None, N_HEADS_A, 2 * t_new, LANES), per_batch),
            pl.BlockSpec((None, N_HEADS_A, PAGE_SIZE, LANES), per_batch),
            pl.BlockSpec((None, N_HEADS_A, PAGE_SIZE, LANES), per_batch),
            pl.BlockSpec((None, 1, DV), lambda b, p, pt: (layer, 0, 0)),
        ] + kv_specs + kv_specs,
        out_specs=pl.BlockSpec((t_new, width), lambda b, p, pt: (b, 0)),
        scratch_shapes=[pltpu.VMEM((N_HEADS_A, 2 * t_new, 1), F32),
                        pltpu.VMEM((N_HEADS_A, 2 * t_new, 1), F32),
                        pltpu.VMEM((N_HEADS_A, 2 * t_new, DV), F32)],
    )
    return pl.pallas_call(
        kern,
        grid_spec=grid_spec,
        out_shape=jax.ShapeDtypeStruct((db * t_new, width), F32),
        compiler_params=pltpu.CompilerParams(
            dimension_semantics=("parallel", "arbitrary"), vmem_limit_bytes=VMEM_LIMIT),
        name="attn_sample",
    )(page_table, lam, qm, knew, vnew, subln, *([cache_k] * len(kv_specs)), *([cache_v] * len(kv_specs)))


def _post_kernel(h_ref, a_ref, up_ref, halo_ref, ug_ref, vg_ref, pw_ref, ps_ref, ws_ref, gb_ref,
                 wo_ref, o_ref, ext_ref, cat_ref, *, tm, tiles_per_seq, d_attn, d_pool):
    i = pl.program_id(0)

    halo = halo_ref[...]
    if tiles_per_seq is not None:
        seq_tile = i % tiles_per_seq
        halo = jnp.where(seq_tile == 0, jnp.zeros_like(halo), halo)
        pos = seq_tile * tm + lax.broadcasted_iota(jnp.int32, (tm, LANES), 0)
    ext_ref[0:HALO, :] = halo
    ext_ref[HALO:HALO + tm, :] = up_ref[...]
    for g, w in enumerate(POOL_WINDOWS):
        sl = slice(g * LANES, (g + 1) * LANES)
        u = up_ref[:, sl]
        tot = u
        for k in range(1, w):
            tot = tot + ext_ref[HALO - k:HALO - k + tm, sl]
        if tiles_per_seq is not None:
            cnt = jnp.minimum(w, pos + 1).astype(F32)
            mean = tot / cnt
        else:
            mean = tot / float(w)
        y = _dot((mean - u).astype(BF16), pw_ref[g]) * ps_ref[:, sl]
        cat_ref[:, d_attn + g * LANES:d_attn + (g + 1) * LANES] = y.astype(BF16)

    r = lax.broadcasted_iota(jnp.int32, (CHUNK, CHUNK), 0)
    c = lax.broadcasted_iota(jnp.int32, (CHUNK, CHUNK), 1)
    o_g = d_attn + d_pool
    for hh in range(N_HEADS_C):
        wsc = jnp.where(r >= c, ws_ref[hh], 0.0).astype(BF16)
        bias = gb_ref[hh]
        cs = slice(hh * CHUNK, (hh + 1) * CHUNK)
        for n in range(tm // CHUNK):
            rs = slice(n * CHUNK, (n + 1) * CHUNK)
            gate = _dot(wsc, vg_ref[rs, cs].astype(BF16)) + bias
            cat_ref[rs, o_g + hh * CHUNK:o_g + (hh + 1) * CHUNK] = (ug_ref[rs, cs] * gate).astype(BF16)

    cat_ref[:, 0:d_attn] = a_ref[...]
    o_ref[...] = h_ref[...] + _dot(cat_ref[...], wo_ref[...])


def _post(h, a, up, halo_src, ug, vg, pool_w, pool_scale, ws, gb, w_out, layer, tm, tiles_per_seq):
    m, d = h.shape
    d_attn = a.shape[1]
    d_pool = up.shape[1]
    d_gmlp = ug.shape[1]
    row = lambda i: (i, 0)
    if tiles_per_seq is not None:
        halo_map = lambda i: (jnp.maximum(i * (tm // HALO) - 1, 0), 0)
    else:
        halo_map = row
    kern = functools.partial(_post_kernel, tm=tm, tiles_per_seq=tiles_per_seq, d_attn=d_attn,
                             d_pool=d_pool)
    return pl.pallas_call(
        kern,
        grid=(m // tm,),
        in_specs=[
            pl.BlockSpec((tm, d), row),
            pl.BlockSpec((tm, d_attn), row),
            pl.BlockSpec((tm, d_pool), row),
            pl.BlockSpec((HALO, d_pool), halo_map),
            pl.BlockSpec((tm, d_gmlp), row),
            pl.BlockSpec((tm, d_gmlp), row),
            pl.BlockSpec((None, len(POOL_WINDOWS), LANES, LANES), lambda i: (layer, 0, 0, 0)),
            pl.BlockSpec((None, 1, d_pool), lambda i: (layer, 0, 0)),
            pl.BlockSpec((None, N_HEADS_C, CHUNK, CHUNK), lambda i: (layer, 0, 0, 0)),
            pl.BlockSpec((None, N_HEADS_C, CHUNK, CHUNK), lambda i: (layer, 0, 0, 0)),
            pl.BlockSpec((None, d, d), lambda i: (layer, 0, 0)),
        ],
        out_specs=pl.BlockSpec((tm, d), row),
        out_shape=jax.ShapeDtypeStruct((m, d), F32),
        scratch_shapes=[pltpu.VMEM((HALO + tm, d_pool), F32), pltpu.VMEM((tm, d), BF16)],
        compiler_params=pltpu.CompilerParams(
            dimension_semantics=("parallel",), vmem_limit_bytes=VMEM_LIMIT),
        name="post",
    )(h, a, up, halo_src, ug, vg, pool_w, pool_scale, ws, gb, w_out)


def _rope_tables(pos):
    half = DK // 2
    inv_freq = jnp.power(ROPE_THETA, -jnp.arange(half, dtype=F32) / half)
    ang = pos.astype(F32)[:, None] * inv_freq[None, :]
    cos = jnp.tile(jnp.cos(ang), (1, LANES // half))
    sign = jnp.where((jnp.arange(LANES) % DK) < half, -1.0, 1.0).astype(F32)
    sin = jnp.tile(jnp.sin(ang), (1, LANES // half)) * sign[None, :]
    return cos, sin


def kernel(x_prompt, x_sample, cache_k, cache_v, state_pool, page_table, norm_ffn1, w1_gate, w1_up, w1_down, norm_mix, w_in, w_out, q_norm, k_norm, lambda_q1, lambda_k1, lambda_q2, lambda_k2, subln, pool_w, pool_scale, gmlp_norm, gmlp_ws, gmlp_b, norm_ffn2, w2_gate, w2_up, w2_down):
    b, seq, d = x_prompt.shape
    db, t_new, _ = x_sample.shape
    depth = w_in.shape[0]
    n_pages = page_table.shape[1]
    past = n_pages * PAGE_SIZE
    width = N_HEADS_A * DV
    d_pool = pool_scale.shape[1]
    d_gmlp = gmlp_norm.shape[1]
    ms = db * t_new

    tm_ffn, tf = 1024, 256
    tm_proj = 256
    t_attn = 512
    tm_post = 256
    g_pages = 8

    bf = lambda w: w.astype(BF16)
    w1g, w1u, w1d = w1_gate, w1_up, w1_down
    w2g, w2u, w2d = w2_gate, w2_up, w2_down
    w_in_b, w_out_b, pool_w_b = bf(w_in), bf(w_out), bf(pool_w)
    row3 = lambda p: p.reshape(depth, 1, -1)
    n1, nm, n2 = row3(norm_ffn1), row3(norm_mix), row3(norm_ffn2)
    qg = row3(jnp.tile(q_norm, (1, LANES // DK)))
    kg = row3(jnp.tile(k_norm, (1, LANES // DK)))
    gg, sub, ps = row3(gmlp_norm), row3(subln), row3(pool_scale)
    lam = jnp.stack([lambda_q1, lambda_k1, lambda_q2, lambda_k2], axis=1)
    gb = jnp.broadcast_to(gmlp_b[:, :, :, None], gmlp_b.shape + (CHUNK,))
    grp = jnp.arange(LANES) // DK
    avg = jnp.where(grp[:, None] == grp[None, :], 1.0 / DK, 0.0).astype(BF16)

    cos_p, sin_p = _rope_tables(jnp.arange(seq, dtype=jnp.int32))
    cos_s, sin_s = _rope_tables(past + (jnp.arange(ms, dtype=jnp.int32) % t_new))


    ck = cache_k.reshape(depth, cache_k.shape[1], PAGE_SIZE * N_HEADS_A, LANES)
    cv = cache_v.reshape(depth, cache_v.shape[1], PAGE_SIZE * N_HEADS_A, LANES)
    new_pad = ((0, 0), (0, 0), (0, PAGE_SIZE - t_new), (0, 0))

    hp = x_prompt.reshape(b * seq, d)
    hs = x_sample.reshape(ms, d)
    kp_l, vp_l, poolp_l, ks_l, vs_l, pools_l, gv_l = [], [], [], [], [], [], []
    pad_rows = lambda x: jnp.pad(x.reshape(db, t_new, -1), ((0, 0), (0, CHUNK - t_new), (0, 0))
                                 ).reshape(db * CHUNK, -1)

    for l in range(depth):
        hp, hs = _ffn(hp, hs, n1, w1g, w1u, w1d, l, tm_ffn, tf)

        q, k, kb, v, vb, up, ug, vg = _proj(hp, nm, w_in_b, qg, kg, gg, cos_p, sin_p, avg, l,
                                            tm_proj, seq // tm_proj)
        a = _attn_prompt(lam, q, kb, vb, sub, l, b, seq, t_attn)
        hp = _post(hp, a, up, up, ug, vg, pool_w_b, ps, gmlp_ws, gb, w_out_b, l, tm_post,
                   seq // tm_post)
        kp_l.append(k)
        vp_l.append(v)
        poolp_l.append(up.reshape(b, seq, d_pool)[:, seq - POOL_BUF:])

        q, k, kb, v, vb, us, ug, vg = _proj(hs, nm, w_in_b, qg, kg, gg, cos_s, sin_s, avg, l, ms, 1)
        q5 = q.reshape(db, t_new, N_HEADS_A, 2, DK).transpose(0, 2, 1, 3, 4)
        comp = jnp.arange(2)
        qm = jnp.where(comp[None, None, :, None, None, None] == comp[None, None, None, None, :, None],
                       q5[:, :, None], jnp.zeros((), BF16))
        qm = qm.reshape(db, N_HEADS_A, 2 * t_new, LANES)
        knew = jnp.pad(kb.reshape(db, t_new, N_HEADS_A, LANES).transpose(0, 2, 1, 3), new_pad)
        vnew = jnp.pad(vb.reshape(db, t_new, N_HEADS_A, LANES).transpose(0, 2, 1, 3), new_pad)
        a = _attn_sample(page_table, lam, qm, knew, vnew, sub, ck, cv, l, g_pages, t_new)
        ext = jnp.concatenate([state_pool[l], us.reshape(db, t_new, d_pool)], axis=1)
        halo = jnp.pad(state_pool[l], ((0, 0), (HALO - POOL_BUF, 0), (0, 0))).reshape(db * HALO, d_pool)
        hs_pad = _post(pad_rows(hs), pad_rows(a.astype(BF16)), pad_rows(us), halo, pad_rows(ug),
                       pad_rows(vg), pool_w_b, ps, gmlp_ws, gb, w_out_b, l, CHUNK, None)
        hs = hs_pad.reshape(db, CHUNK, d)[:, :t_new].reshape(ms, d)
        ks_l.append(k)
        vs_l.append(v)
        pools_l.append(ext[:, t_new:])
        gv_l.append(vg)

        hp, hs = _ffn(hp, hs, n2, w2g, w2u, w2d, l, tm_ffn, tf)

    n_pg = seq // PAGE_SIZE
    return (
        hp.reshape(b, seq, d),
        hs.reshape(db, t_new, d),
        jnp.stack(kp_l).reshape(depth, b, n_pg, PAGE_SIZE, N_HEADS_A, 2 * DK),
        jnp.stack(vp_l).reshape(depth, b, n_pg, PAGE_SIZE, N_HEADS_A, DV),
        jnp.stack(poolp_l),
        jnp.stack(ks_l).reshape(depth, db, t_new, N_HEADS_A, 2 * DK),
        jnp.stack(vs_l).reshape(depth, db, t_new, N_HEADS_A, DV),
        jnp.stack(pools_l),
        jnp.stack(gv_l).reshape(depth, db, t_new, d_gmlp),
    )
```

```python
import functools
import math

import jax
import jax.numpy as jnp
from jax import lax
from jax.experimental import pallas as pl
from jax.experimental.pallas import tpu as pltpu

F32 = jnp.float32
BF16 = jnp.bfloat16

EPS = 1e-6
ROPE_THETA = 10000.0
N_HEADS_A = 8
DV = 128
DK = 64
POOL_WINDOWS = (2, 4, 8, 16)
POOL_BUF = 15
HALO = 16
CHUNK = 128
N_HEADS_C = 4
PAGE_SIZE = 128
LANES = 128
NEG = -1e30
VMEM_LIMIT = 56 * 1024 * 1024

_NT = (((1,), (1,)), ((), ()))


def _dot(a, b):
    return jnp.dot(a, b, preferred_element_type=F32)


def _dot_nt(a, b):
    return lax.dot_general(a, b, _NT, preferred_element_type=F32)


def _rms(x, g):
    ms = jnp.mean(x * x, axis=-1, keepdims=True)
    return x * lax.rsqrt(ms + EPS) * g


def _gelu(x):
    return 0.5 * x * (1.0 + lax.erf(x * (2.0 ** -0.5)))


def _l_init(layer):
    return 0.8 - 0.6 * math.exp(-0.3 * layer)


def _lam(lam_ref, l_init):
    lp = lam_ref[...]
    s1 = jnp.sum(lp[0:1] * lp[1:2], axis=-1, keepdims=True)
    s2 = jnp.sum(lp[2:3] * lp[3:4], axis=-1, keepdims=True)
    return jnp.exp(s1) - jnp.exp(s2) + l_init


def _ffn_kernel(x_ref, xs_ref, g_ref, wg_ref, wu_ref, wd_ref, o_ref, os_ref, xn_ref, *, tm):
    i = pl.program_id(0)
    j = pl.program_id(1)
    last_j = pl.num_programs(1) - 1

    def hidden(xn):
        gate = _dot(xn, wg_ref[...].astype(BF16))
        up = _dot(xn, wu_ref[...].astype(BF16))
        return (gate * jax.nn.sigmoid(gate) * up).astype(BF16)

    @pl.when(j == 0)
    def _():
        xn_ref[0:tm, :] = _rms(x_ref[...], g_ref[...]).astype(BF16)
        o_ref[...] = jnp.zeros_like(o_ref)

    @pl.when((j == 0) & (i == 0))
    def _():
        xn_ref[tm:, :] = _rms(xs_ref[...], g_ref[...]).astype(BF16)
        os_ref[...] = jnp.zeros_like(os_ref)

    @pl.when(i == 0)
    def _():
        act = hidden(xn_ref[...])
        wd = wd_ref[...].astype(BF16)
        o_ref[...] += _dot(act[:tm], wd)
        os_ref[...] += _dot(act[tm:], wd)

    @pl.when(i > 0)
    def _():
        o_ref[...] += _dot(hidden(xn_ref[0:tm, :]), wd_ref[...].astype(BF16))

    @pl.when(j == last_j)
    def _():
        o_ref[...] = x_ref[...] + 0.5 * o_ref[...]

    @pl.when((j == last_j) & (i == 0))
    def _():
        os_ref[...] = xs_ref[...] + 0.5 * os_ref[...]


def _ffn(x, xs, g, wg, wu, wd, layer, tm, tf):
    m, d = x.shape
    ms = xs.shape[0]
    f = wg.shape[-1]
    return pl.pallas_call(
        functools.partial(_ffn_kernel, tm=tm),
        grid=(m // tm, f // tf),
        in_specs=[
            pl.BlockSpec((tm, d), lambda i, j: (i, 0)),
            pl.BlockSpec((ms, d), lambda i, j: (0, 0)),
            pl.BlockSpec((None, 1, d), lambda i, j: (layer, 0, 0)),
            pl.BlockSpec((None, d, tf), lambda i, j: (layer, 0, j)),
            pl.BlockSpec((None, d, tf), lambda i, j: (layer, 0, j)),
            pl.BlockSpec((None, tf, d), lambda i, j: (layer, j, 0)),
        ],
        out_specs=[pl.BlockSpec((tm, d), lambda i, j: (i, 0)),
                   pl.BlockSpec((ms, d), lambda i, j: (0, 0))],
        out_shape=[jax.ShapeDtypeStruct((m, d), F32), jax.ShapeDtypeStruct((ms, d), F32)],
        scratch_shapes=[pltpu.VMEM((tm + ms, d), BF16)],
        compiler_params=pltpu.CompilerParams(
            dimension_semantics=("arbitrary", "arbitrary"), vmem_limit_bytes=VMEM_LIMIT),
        name="ffn",
    )(x, xs, g, wg, wu, wd)


def _proj_kernel(x_ref, g_ref, w_ref, qg_ref, kg_ref, gg_ref, cos_ref, sin_ref, avg_ref,
                 q_ref, k_ref, kb_ref, v_ref, vb_ref, up_ref, ug_ref, vg_ref,
                 *, d_attn, d_pool, d_gmlp, attn_scale):
    xn = _rms(x_ref[...], g_ref[...]).astype(BF16)
    tm = xn.shape[0]
    cos = cos_ref[...]
    sin = sin_ref[...]
    avg = avg_ref[...]
    lane = lax.broadcasted_iota(jnp.int32, (tm, LANES), 1)
    first_half = (lane % DK) < (DK // 2)

    def norm_rope(z, gain):
        z2 = z * z
        hi = z2.astype(BF16)
        lo = (z2 - hi.astype(F32)).astype(BF16)
        ms = _dot(hi, avg) + _dot(lo, avg)
        y = z * lax.rsqrt(ms + EPS) * gain
        partner = jnp.where(first_half, pltpu.roll(y, LANES - DK // 2, 1), pltpu.roll(y, DK // 2, 1))
        return y * cos + partner * sin

    zq = _dot(xn, w_ref[:, 0:d_attn])
    qg = qg_ref[...]
    for c in range(d_attn // LANES):
        sl = slice(c * LANES, (c + 1) * LANES)
        q_ref[:, sl] = (norm_rope(zq[:, sl], qg) * attn_scale).astype(BF16)

    zk = _dot(xn, w_ref[:, d_attn:2 * d_attn])
    kg = kg_ref[...]
    for c in range(d_attn // LANES):
        sl = slice(c * LANES, (c + 1) * LANES)
        kr = norm_rope(zk[:, sl], kg)
        k_ref[:, sl] = kr
        kb_ref[:, sl] = kr.astype(BF16)

    zv = _dot(xn, w_ref[:, 2 * d_attn:3 * d_attn])
    v_ref[...] = zv
    vb_ref[...] = zv.astype(BF16)

    o = 3 * d_attn
    up_ref[...] = _dot(xn, w_ref[:, o:o + d_pool])
    o += d_pool
    ug_ref[...] = _gelu(_dot(xn, w_ref[:, o:o + d_gmlp]))
    o += d_gmlp
    zg = _gelu(_dot(xn, w_ref[:, o:o + d_gmlp]))
    vg_ref[...] = _rms(zg, gg_ref[...])


def _proj(x, g, w_in, qg, kg, gg, cos, sin, avg, layer, tm, table_tiles):
    m, d = x.shape
    d_in = w_in.shape[-1]
    d_attn = N_HEADS_A * DV
    d_pool = 4 * LANES
    d_gmlp = N_HEADS_C * CHUNK
    row = lambda i: (i, 0)
    tab = lambda i: (i % table_tiles, 0)
    par = lambda i: (layer, 0, 0)
    kern = functools.partial(_proj_kernel, d_attn=d_attn, d_pool=d_pool, d_gmlp=d_gmlp,
                             attn_scale=DK ** -0.5 * math.log2(math.e))
    return pl.pallas_call(
        kern,
        grid=(m // tm,),
        in_specs=[
            pl.BlockSpec((tm, d), row),
            pl.BlockSpec((None, 1, d), par),
            pl.BlockSpec((None, d, d_in), par),
            pl.BlockSpec((None, 1, LANES), par),
            pl.BlockSpec((None, 1, LANES), par),
            pl.BlockSpec((None, 1, d_gmlp), par),
            pl.BlockSpec((tm, LANES), tab),
            pl.BlockSpec((tm, LANES), tab),
            pl.BlockSpec((LANES, LANES), lambda i: (0, 0)),
        ],
        out_specs=[
            pl.BlockSpec((tm, d_attn), row),
            pl.BlockSpec((tm, d_attn), row),
            pl.BlockSpec((tm, d_attn), row),
            pl.BlockSpec((tm, d_attn), row),
            pl.BlockSpec((tm, d_attn), row),
            pl.BlockSpec((tm, d_pool), row),
            pl.BlockSpec((tm, d_gmlp), row),
            pl.BlockSpec((tm, d_gmlp), row),
        ],
        out_shape=[
            jax.ShapeDtypeStruct((m, d_attn), BF16),
            jax.ShapeDtypeStruct((m, d_attn), F32),
            jax.ShapeDtypeStruct((m, d_attn), BF16),
            jax.ShapeDtypeStruct((m, d_attn), F32),
            jax.ShapeDtypeStruct((m, d_attn), BF16),
            jax.ShapeDtypeStruct((m, d_pool), F32),
            jax.ShapeDtypeStruct((m, d_gmlp), F32),
            jax.ShapeDtypeStruct((m, d_gmlp), F32),
        ],
        compiler_params=pltpu.CompilerParams(
            dimension_semantics=("parallel",), vmem_limit_bytes=VMEM_LIMIT),
        name="proj",
    )(x, g, w_in, qg, kg, gg, cos, sin, avg)


def _attn_kernel(lam_ref, q_ref, k_ref, v_ref, sub_ref, o_ref, *scratch, t, nh, l_init):
    qi = pl.program_id(2)
    per_head = [scratch[6 * hh:6 * hh + 6] for hh in range(nh)]

    @pl.when(qi == 0)
    def _():
        for hh, (vt_ref, *_) in enumerate(per_head):
            for jb in range(vt_ref.shape[0]):
                vt_ref[jb] = v_ref[jb * t:(jb + 1) * t, hh * DV:(hh + 1) * DV].astype(F32).T.astype(BF16)

    lane = lax.broadcasted_iota(jnp.int32, (t, LANES), 1)
    q2s = []
    for hh, (_, m_ref, l_ref, acc_ref, _, _) in enumerate(per_head):
        q = q_ref[:, hh * LANES:(hh + 1) * LANES]
        zero = jnp.zeros_like(q)
        q2s.append(jnp.concatenate([jnp.where(lane < DK, q, zero), jnp.where(lane >= DK, q, zero)], axis=0))
        m_ref[...] = jnp.full_like(m_ref, NEG)
        l_ref[...] = jnp.zeros_like(l_ref)
        acc_ref[...] = jnp.zeros_like(acc_ref)

    def scores(hh, j):
        kj = k_ref[pl.ds(pl.multiple_of(j * t, t), t), hh * LANES:(hh + 1) * LANES]
        return _dot_nt(kj, q2s[hh])

    def update(hh, j, s, masked):
        vt_ref, m_ref, l_ref, acc_ref, _, _ = per_head[hh]
        if masked:
            r = lax.broadcasted_iota(jnp.int32, s.shape, 0)
            c = lax.broadcasted_iota(jnp.int32, s.shape, 1)
            s = jnp.where(r <= jnp.where(c >= t, c - t, c), s, NEG)
        m_old = m_ref[...]
        m_new = jnp.maximum(m_old, jnp.max(s, axis=0, keepdims=True))
        p = jnp.exp2(s - m_new)
        alpha = jnp.exp2(m_old - m_new)
        l_ref[...] = alpha * l_ref[...] + jnp.sum(p, axis=0, keepdims=True)
        acc_ref[...] = alpha * acc_ref[...] + _dot(vt_ref[j], p.astype(BF16))
        m_ref[...] = m_new

    heads = range(nh)
    sa = [ph[4] for ph in per_head]
    sb = [ph[5] for ph in per_head]

    def body(i2, carry):
        j = 2 * i2
        for hh in heads:
            sb[hh][...] = scores(hh, j + 1)
        for hh in heads:
            update(hh, j, sa[hh][...], False)
        for hh in heads:
            sa[hh][...] = scores(hh, j + 2)
        for hh in heads:
            update(hh, j + 1, sb[hh][...], False)
        return carry

    for hh in heads:
        sa[hh][...] = scores(hh, 0)
    lax.fori_loop(0, qi // 2, body, 0)

    @pl.when(qi % 2 == 0)
    def _():
        for hh in heads:
            update(hh, qi, sa[hh][...], True)

    @pl.when(qi % 2 == 1)
    def _():
        for hh in heads:
            sb[hh][...] = scores(hh, qi)
        for hh in heads:
            update(hh, qi - 1, sa[hh][...], False)
        for hh in heads:
            update(hh, qi, sb[hh][...], True)

    lam = _lam(lam_ref, l_init)
    for hh, (_, _, l_ref, acc_ref, _, _) in enumerate(per_head):
        o = acc_ref[...] / l_ref[...]
        o = o[:, :t] - lam * o[:, t:]
        ms = jnp.mean(o * o, axis=0, keepdims=True)
        y = (o * lax.rsqrt(ms + EPS)).T
        o_ref[:, hh * DV:(hh + 1) * DV] = (y * sub_ref[...] * (1.0 - l_init)).astype(o_ref.dtype)


def _attn_prompt(lam, q, k, v, subln, layer, batch, seq, t, nh):
    nq = seq // t
    kern = functools.partial(_attn_kernel, t=t, nh=nh, l_init=_l_init(layer))
    head_scratch = [pltpu.VMEM((nq, DV, t), BF16),
                    pltpu.VMEM((1, 2 * t), F32), pltpu.VMEM((1, 2 * t), F32),
                    pltpu.VMEM((DV, 2 * t), F32),
                    pltpu.VMEM((t, 2 * t), F32), pltpu.VMEM((t, 2 * t), F32)]
    return pl.pallas_call(
        kern,
        grid=(batch, N_HEADS_A // nh, nq),
        in_specs=[
            pl.BlockSpec((None, 4, DK), lambda b, h, i: (layer, 0, 0)),
            pl.BlockSpec((t, nh * LANES), lambda b, h, i: (b * nq + i, h)),
            pl.BlockSpec((seq, nh * LANES), lambda b, h, i: (b, h)),
            pl.BlockSpec((seq, nh * LANES), lambda b, h, i: (b, h)),
            pl.BlockSpec((None, 1, DV), lambda b, h, i: (layer, 0, 0)),
        ],
        out_specs=pl.BlockSpec((t, nh * LANES), lambda b, h, i: (b * nq + i, h)),
        out_shape=jax.ShapeDtypeStruct((batch * seq, N_HEADS_A * DV), BF16),
        scratch_shapes=head_scratch * nh,
        compiler_params=pltpu.CompilerParams(
            dimension_semantics=("parallel", "parallel", "arbitrary"), vmem_limit_bytes=VMEM_LIMIT),
        name="attn_prompt",
    )(lam, q, k, v, subln)


def _dec_kernel(pt_ref, lam_ref, qm_ref, knew_ref, vnew_ref, sub_ref, *rest, g_pages, t_new, l_init):
    del pt_ref
    k_refs = rest[:g_pages]
    v_refs = rest[g_pages:2 * g_pages]
    o_ref, m_ref, l_ref, acc_ref = rest[2 * g_pages:]
    pg = pl.program_id(1)

    @pl.when(pg == 0)
    def _():
        m_ref[...] = jnp.full_like(m_ref, NEG)
        l_ref[...] = jnp.zeros_like(l_ref)
        acc_ref[...] = jnp.zeros_like(acc_ref)

    qm = qm_ref[...]

    def by_head(refs):
        return jnp.stack([
            jnp.concatenate([r[pl.ds(h, PAGE_SIZE, stride=N_HEADS_A), :] for r in refs], axis=0)
            for h in range(N_HEADS_A)]).astype(BF16)

    def update(s, v3):
        m_old = m_ref[...]
        m_new = jnp.maximum(m_old, jnp.max(s, axis=-1, keepdims=True))
        p = jnp.exp2(s - m_new)
        alpha = jnp.exp2(m_old - m_new)
        l_ref[...] = alpha * l_ref[...] + jnp.sum(p, axis=-1, keepdims=True)
        pv = jnp.einsum('hqk,hkd->hqd', p.astype(BF16), v3, preferred_element_type=F32)
        acc_ref[...] = alpha * acc_ref[...] + pv
        m_ref[...] = m_new

    def scores(k3):
        return jnp.einsum('hqd,hkd->hqk', qm, k3, preferred_element_type=F32)

    update(scores(by_head(k_refs)), by_head(v_refs))

    @pl.when(pg == pl.num_programs(1) - 1)
    def _():
        s_new = scores(knew_ref[...])
        r = lax.broadcasted_iota(jnp.int32, s_new.shape, 1)
        c = lax.broadcasted_iota(jnp.int32, s_new.shape, 2)
        update(jnp.where(c <= (r % t_new), s_new, NEG), vnew_ref[...])

        o = acc_ref[...] / l_ref[...]
        o = o[:, :t_new] - _lam(lam_ref, l_init) * o[:, t_new:]
        y = _rms(o, sub_ref[...]) * (1.0 - l_init)
        for h in range(N_HEADS_A):
            o_ref[:, h * DV:(h + 1) * DV] = y[h]


def _attn_sample(page_table, lam, qm, knew, vnew, subln, cache_k, cache_v, layer, g_pages, t_new):
    db, n_pages = page_table.shape
    width = N_HEADS_A * DV
    page_rows = PAGE_SIZE * N_HEADS_A
    kern = functools.partial(_dec_kernel, g_pages=g_pages, t_new=t_new, l_init=_l_init(layer))

    def page_spec(g):
        return pl.BlockSpec((None, None, page_rows, LANES),
                            lambda b, p, pt: (layer, pt[b, p * g_pages + g], 0, 0))

    kv_specs = [page_spec(g) for g in range(g_pages)]

    per_batch = lambda b, p, pt: (b, 0, 0, 0)
    grid_spec = pltpu.PrefetchScalarGridSpec(
        num_scalar_prefetch=1,
        grid=(db, n_pages // g_pages),
        in_specs=[
            pl.BlockSpec((None, 4, DK), lambda b, p, pt: (layer, 0, 0)),
            pl.BlockSpec((None, N_HEADS_A, 2 * t_new, LANES), per_batch),
            pl.BlockSpec((None, N_HEADS_A, PAGE_SIZE, LANES), per_batch),
            pl.BlockSpec((None, N_HEADS_A, PAGE_SIZE, LANES), per_batch),
            pl.BlockSpec((None, 1, DV), lambda b, p, pt: (layer, 0, 0)),
        ] + kv_specs + kv_specs,
        out_specs=pl.BlockSpec((t_new, width), lambda b, p, pt: (b, 0)),
        scratch_shapes=[pltpu.VMEM((N_HEADS_A, 2 * t_new, 1), F32),
                        pltpu.VMEM((N_HEADS_A, 2 * t_new, 1), F32),
                        pltpu.VMEM((N_HEADS_A, 2 * t_new, DV), F32)],
    )
    return pl.pallas_call(
        kern,
        grid_spec=grid_spec,
        out_shape=jax.ShapeDtypeStruct((db * t_new, width), F32),
        compiler_params=pltpu.CompilerParams(
            dimension_semantics=("parallel", "arbitrary"), vmem_limit_bytes=VMEM_LIMIT),
        name="attn_sample",
    )(page_table, lam, qm, knew, vnew, subln, *([cache_k] * len(kv_specs)), *([cache_v] * len(kv_specs)))


def _post_kernel(h_ref, a_ref, up_ref, halo_ref, ug_ref, vg_ref, pw_ref, ps_ref, ws_ref, gb_ref,
                 wo_ref, o_ref, ext_ref, cat_ref, *, tm, tiles_per_seq, d_attn, d_pool):
    i = pl.program_id(0)

    halo = halo_ref[...]
    if tiles_per_seq is not None:
        seq_tile = i % tiles_per_seq
        halo = jnp.where(seq_tile == 0, jnp.zeros_like(halo), halo)
        pos = seq_tile * tm + lax.broadcasted_iota(jnp.int32, (tm, LANES), 0)
    ext_ref[0:HALO, :] = halo
    ext_ref[HALO:HALO + tm, :] = up_ref[...]
    for g, w in enumerate(POOL_WINDOWS):
        sl = slice(g * LANES, (g + 1) * LANES)
        u = up_ref[:, sl]
        tot = u
        for k in range(1, w):
            tot = tot + ext_ref[HALO - k:HALO - k + tm, sl]
        if tiles_per_seq is not None:
            cnt = jnp.minimum(w, pos + 1).astype(F32)
            mean = tot / cnt
        else:
            mean = tot / float(w)
        y = _dot((mean - u).astype(BF16), pw_ref[g]) * ps_ref[:, sl]
        cat_ref[:, d_attn + g * LANES:d_attn + (g + 1) * LANES] = y.astype(BF16)

    r = lax.broadcasted_iota(jnp.int32, (CHUNK, CHUNK), 0)
    c = lax.broadcasted_iota(jnp.int32, (CHUNK, CHUNK), 1)
    o_g = d_attn + d_pool
    for hh in range(N_HEADS_C):
        wsc = jnp.where(r >= c, ws_ref[hh], 0.0).astype(BF16)
        bias = gb_ref[hh]
        cs = slice(hh * CHUNK, (hh + 1) * CHUNK)
        for n in range(tm // CHUNK):
            rs = slice(n * CHUNK, (n + 1) * CHUNK)
            gate = _dot(wsc, vg_ref[rs, cs].astype(BF16)) + bias
            cat_ref[rs, o_g + hh * CHUNK:o_g + (hh + 1) * CHUNK] = (ug_ref[rs, cs] * gate).astype(BF16)

    cat_ref[:, 0:d_attn] = a_ref[...]
    o_ref[...] = h_ref[...] + _dot(cat_ref[...], wo_ref[...])


def _post(h, a, up, halo_src, ug, vg, pool_w, pool_scale, ws, gb, w_out, layer, tm, tiles_per_seq):
    m, d = h.shape
    d_attn = a.shape[1]
    d_pool = up.shape[1]
    d_gmlp = ug.shape[1]
    row = lambda i: (i, 0)
    if tiles_per_seq is not None:
        halo_map = lambda i: (jnp.maximum(i * (tm // HALO) - 1, 0), 0)
    else:
        halo_map = row
    kern = functools.partial(_post_kernel, tm=tm, tiles_per_seq=tiles_per_seq, d_attn=d_attn,
                             d_pool=d_pool)
    return pl.pallas_call(
        kern,
        grid=(m // tm,),
        in_specs=[
            pl.BlockSpec((tm, d), row),
            pl.BlockSpec((tm, d_attn), row),
            pl.BlockSpec((tm, d_pool), row),
            pl.BlockSpec((HALO, d_pool), halo_map),
            pl.BlockSpec((tm, d_gmlp), row),
            pl.BlockSpec((tm, d_gmlp), row),
            pl.BlockSpec((None, len(POOL_WINDOWS), LANES, LANES), lambda i: (layer, 0, 0, 0)),
            pl.BlockSpec((None, 1, d_pool), lambda i: (layer, 0, 0)),
            pl.BlockSpec((None, N_HEADS_C, CHUNK, CHUNK), lambda i: (layer, 0, 0, 0)),
            pl.BlockSpec((None, N_HEADS_C, CHUNK, CHUNK), lambda i: (layer, 0, 0, 0)),
            pl.BlockSpec((None, d, d), lambda i: (layer, 0, 0)),
        ],
        out_specs=pl.BlockSpec((tm, d), row),
        out_shape=jax.ShapeDtypeStruct((m, d), F32),
        scratch_shapes=[pltpu.VMEM((HALO + tm, d_pool), F32), pltpu.VMEM((tm, d), BF16)],
        compiler_params=pltpu.CompilerParams(
            dimension_semantics=("parallel",), vmem_limit_bytes=VMEM_LIMIT),
        name="post",
    )(h, a, up, halo_src, ug, vg, pool_w, pool_scale, ws, gb, w_out)


def _rope_tables(pos):
    half = DK // 2
    inv_freq = jnp.power(ROPE_THETA, -jnp.arange(half, dtype=F32) / half)
    ang = pos.astype(F32)[:, None] * inv_freq[None, :]
    cos = jnp.tile(jnp.cos(ang), (1, LANES // half))
    sign = jnp.where((jnp.arange(LANES) % DK) < half, -1.0, 1.0).astype(F32)
    sin = jnp.tile(jnp.sin(ang), (1, LANES // half)) * sign[None, :]
    return cos, sin


def kernel(x_prompt, x_sample, cache_k, cache_v, state_pool, page_table, norm_ffn1, w1_gate, w1_up, w1_down, norm_mix, w_in, w_out, q_norm, k_norm, lambda_q1, lambda_k1, lambda_q2, lambda_k2, subln, pool_w, pool_scale, gmlp_norm, gmlp_ws, gmlp_b, norm_ffn2, w2_gate, w2_up, w2_down):
    b, seq, d = x_prompt.shape
    db, t_new, _ = x_sample.shape
    depth = w_in.shape[0]
    n_pages = page_table.shape[1]
    past = n_pages * PAGE_SIZE
    width = N_HEADS_A * DV
    d_pool = pool_scale.shape[1]
    d_gmlp = gmlp_norm.shape[1]
    ms = db * t_new

    tm_ffn, tf = 1024, 256
    tm_proj = 256
    t_attn = 512
    heads_per_step = 2
    tm_post = 256
    g_pages = 16

    bf = lambda w: w.astype(BF16)
    w1g, w1u, w1d = w1_gate, w1_up, w1_down
    w2g, w2u, w2d = w2_gate, w2_up, w2_down
    w_in_b, w_out_b, pool_w_b = bf(w_in), bf(w_out), bf(pool_w)
    row3 = lambda p: p.reshape(depth, 1, -1)
    n1, nm, n2 = row3(norm_ffn1), row3(norm_mix), row3(norm_ffn2)
    qg = row3(jnp.tile(q_norm, (1, LANES // DK)))
    kg = row3(jnp.tile(k_norm, (1, LANES // DK)))
    gg, sub, ps = row3(gmlp_norm), row3(subln), row3(pool_scale)
    lam = jnp.stack([lambda_q1, lambda_k1, lambda_q2, lambda_k2], axis=1)
    gb = jnp.broadcast_to(gmlp_b[:, :, :, None], gmlp_b.shape + (CHUNK,))
    grp = jnp.arange(LANES) // DK
    avg = jnp.where(grp[:, None] == grp[None, :], 1.0 / DK, 0.0).astype(BF16)

    cos_p, sin_p = _rope_tables(jnp.arange(seq, dtype=jnp.int32))
    cos_s, sin_s = _rope_tables(past + (jnp.arange(ms, dtype=jnp.int32) % t_new))


    ck = cache_k.reshape(depth, cache_k.shape[1], PAGE_SIZE * N_HEADS_A, LANES)
    cv = cache_v.reshape(depth, cache_v.shape[1], PAGE_SIZE * N_HEADS_A, LANES)
    new_pad = ((0, 0), (0, 0), (0, PAGE_SIZE - t_new), (0, 0))

    hp = x_prompt.reshape(b * seq, d)
    hs = x_sample.reshape(ms, d)
    kp_l, vp_l, poolp_l, ks_l, vs_l, pools_l, gv_l = [], [], [], [], [], [], []
    pad_rows = lambda x: jnp.pad(x.reshape(db, t_new, -1), ((0, 0), (0, CHUNK - t_new), (0, 0))
                                 ).reshape(db * CHUNK, -1)

    for l in range(depth):
        hp, hs = _ffn(hp, hs, n1, w1g, w1u, w1d, l, tm_ffn, tf)

        q, k, kb, v, vb, up, ug, vg = _proj(hp, nm, w_in_b, qg, kg, gg, cos_p, sin_p, avg, l,
                                            tm_proj, seq // tm_proj)
        a = _attn_prompt(lam, q, kb, vb, sub, l, b, seq, t_attn, heads_per_step)
        hp = _post(hp, a, up, up, ug, vg, pool_w_b, ps, gmlp_ws, gb, w_out_b, l, tm_post,
                   seq // tm_post)
        kp_l.append(k)
        vp_l.append(v)
        poolp_l.append(up.reshape(b, seq, d_pool)[:, seq - POOL_BUF:])

        q, k, kb, v, vb, us, ug, vg = _proj(hs, nm, w_in_b, qg, kg, gg, cos_s, sin_s, avg, l, ms, 1)
        q5 = q.reshape(db, t_new, N_HEADS_A, 2, DK).transpose(0, 2, 1, 3, 4)
        comp = jnp.arange(2)
        qm = jnp.where(comp[None, None, :, None, None, None] == comp[None, None, None, None, :, None],
                       q5[:, :, None], jnp.zeros((), BF16))
        qm = qm.reshape(db, N_HEADS_A, 2 * t_new, LANES)
        knew = jnp.pad(kb.reshape(db, t_new, N_HEADS_A, LANES).transpose(0, 2, 1, 3), new_pad)
        vnew = jnp.pad(vb.reshape(db, t_new, N_HEADS_A, LANES).transpose(0, 2, 1, 3), new_pad)
        a = _attn_sample(page_table, lam, qm, knew, vnew, sub, ck, cv, l, g_pages, t_new)
        ext = jnp.concatenate([state_pool[l], us.reshape(db, t_new, d_pool)], axis=1)
        halo = jnp.pad(state_pool[l], ((0, 0), (HALO - POOL_BUF, 0), (0, 0))).reshape(db * HALO, d_pool)
        hs_pad = _post(pad_rows(hs), pad_rows(a.astype(BF16)), pad_rows(us), halo, pad_rows(ug),
                       pad_rows(vg), pool_w_b, ps, gmlp_ws, gb, w_out_b, l, CHUNK, None)
        hs = hs_pad.reshape(db, CHUNK, d)[:, :t_new].reshape(ms, d)
        ks_l.append(k)
        vs_l.append(v)
        pools_l.append(ext[:, t_new:])
        gv_l.append(vg)

        hp, hs = _ffn(hp, hs, n2, w2g, w2u, w2d, l, tm_ffn, tf)

    n_pg = seq // PAGE_SIZE
    return (
        hp.reshape(b, seq, d),
        hs.reshape(db, t_new, d),
        jnp.stack(kp_l).reshape(depth, b, n_pg, PAGE_SIZE, N_HEADS_A, 2 * DK),
        jnp.stack(vp_l).reshape(depth, b, n_pg, PAGE_SIZE, N_HEADS_A, DV),
        jnp.stack(poolp_l),
        jnp.stack(ks_l).reshape(depth, db, t_new, N_HEADS_A, 2 * DK),
        jnp.stack(vs_l).reshape(depth, db, t_new, N_HEADS_A, DV),
        jnp.stack(pools_l),
        jnp.stack(gv_l).reshape(depth, db, t_new, d_gmlp),
    )
```

```python
import functools
import math

import jax
import jax.numpy as jnp
from jax import lax
from jax.experimental import pallas as pl
from jax.experimental.pallas import tpu as pltpu

F32 = jnp.float32
BF16 = jnp.bfloat16

EPS = 1e-6
ROPE_THETA = 10000.0
N_HEADS_A = 8
DV = 128
DK = 64
POOL_WINDOWS = (2, 4, 8, 16)
POOL_BUF = 15
HALO = 16
CHUNK = 128
N_HEADS_C = 4
PAGE_SIZE = 128
LANES = 128
NEG = -1e30
VMEM_LIMIT = 56 * 1024 * 1024

_NT = (((1,), (1,)), ((), ()))


def _dot(a, b):
    return jnp.dot(a, b, preferred_element_type=F32)


def _dot_nt(a, b):
    return lax.dot_general(a, b, _NT, preferred_element_type=F32)


def _rms(x, g):
    ms = jnp.mean(x * x, axis=-1, keepdims=True)
    return x * lax.rsqrt(ms + EPS) * g


def _gelu(x):
    return 0.5 * x * (1.0 + lax.erf(x * (2.0 ** -0.5)))


def _l_init(layer):
    return 0.8 - 0.6 * math.exp(-0.3 * layer)


def _lam(lam_ref, l_init):
    lp = lam_ref[...]
    s1 = jnp.sum(lp[0:1] * lp[1:2], axis=-1, keepdims=True)
    s2 = jnp.sum(lp[2:3] * lp[3:4], axis=-1, keepdims=True)
    return jnp.exp(s1) - jnp.exp(s2) + l_init


def _ffn_kernel(x_ref, xs_ref, g_ref, wg_ref, wu_ref, wd_ref, o_ref, os_ref, xn_ref, *, tm):
    i = pl.program_id(0)
    j = pl.program_id(1)
    last_j = pl.num_programs(1) - 1

    def hidden(xn):
        gate = _dot(xn, wg_ref[...].astype(BF16))
        up = _dot(xn, wu_ref[...].astype(BF16))
        return (gate * jax.nn.sigmoid(gate) * up).astype(BF16)

    @pl.when(j == 0)
    def _():
        xn_ref[0:tm, :] = _rms(x_ref[...], g_ref[...]).astype(BF16)
        o_ref[...] = jnp.zeros_like(o_ref)

    @pl.when((j == 0) & (i == 0))
    def _():
        xn_ref[tm:, :] = _rms(xs_ref[...], g_ref[...]).astype(BF16)
        os_ref[...] = jnp.zeros_like(os_ref)

    @pl.when(i == 0)
    def _():
        act = hidden(xn_ref[...])
        wd = wd_ref[...].astype(BF16)
        o_ref[...] += _dot(act[:tm], wd)
        os_ref[...] += _dot(act[tm:], wd)

    @pl.when(i > 0)
    def _():
        o_ref[...] += _dot(hidden(xn_ref[0:tm, :]), wd_ref[...].astype(BF16))

    @pl.when(j == last_j)
    def _():
        o_ref[...] = x_ref[...] + 0.5 * o_ref[...]

    @pl.when((j == last_j) & (i == 0))
    def _():
        os_ref[...] = xs_ref[...] + 0.5 * os_ref[...]


def _ffn(x, xs, g, wg, wu, wd, layer, tm, tf):
    m, d = x.shape
    ms = xs.shape[0]
    f = wg.shape[-1]
    return pl.pallas_call(
        functools.partial(_ffn_kernel, tm=tm),
        grid=(m // tm, f // tf),
        in_specs=[
            pl.BlockSpec((tm, d), lambda i, j: (i, 0)),
            pl.BlockSpec((ms, d), lambda i, j: (0, 0)),
            pl.BlockSpec((None, 1, d), lambda i, j: (layer, 0, 0)),
            pl.BlockSpec((None, d, tf), lambda i, j: (layer, 0, j)),
            pl.BlockSpec((None, d, tf), lambda i, j: (layer, 0, j)),
            pl.BlockSpec((None, tf, d), lambda i, j: (layer, j, 0)),
        ],
        out_specs=[pl.BlockSpec((tm, d), lambda i, j: (i, 0)),
                   pl.BlockSpec((ms, d), lambda i, j: (0, 0))],
        out_shape=[jax.ShapeDtypeStruct((m, d), F32), jax.ShapeDtypeStruct((ms, d), F32)],
        scratch_shapes=[pltpu.VMEM((tm + ms, d), BF16)],
        compiler_params=pltpu.CompilerParams(
            dimension_semantics=("arbitrary", "arbitrary"), vmem_limit_bytes=VMEM_LIMIT),
        name="ffn",
    )(x, xs, g, wg, wu, wd)


def _proj_kernel(x_ref, xs_ref, g_ref, w_ref, qg_ref, kg_ref, gg_ref, cos_ref, sin_ref, coss_ref, sins_ref,
                 avg_ref, *out_refs, tm, d_attn, d_pool, d_gmlp, attn_scale):
    n_out = len(out_refs) // 2
    avg = avg_ref[...]

    def body(x, cos, sin, store):
        rows = x.shape[0]
        xn = _rms(x, g_ref[...]).astype(BF16)
        lane = lax.broadcasted_iota(jnp.int32, (rows, LANES), 1)
        first_half = (lane % DK) < (DK // 2)

        def norm_rope(z, gain):
            z2 = z * z
            hi = z2.astype(BF16)
            lo = (z2 - hi.astype(F32)).astype(BF16)
            ms = _dot(hi, avg) + _dot(lo, avg)
            y = z * lax.rsqrt(ms + EPS) * gain
            partner = jnp.where(first_half, pltpu.roll(y, LANES - DK // 2, 1), pltpu.roll(y, DK // 2, 1))
            return y * cos + partner * sin

        zq = _dot(xn, w_ref[:, 0:d_attn])
        qg = qg_ref[...]
        for c in range(d_attn // LANES):
            sl = slice(c * LANES, (c + 1) * LANES)
            store(0, sl, (norm_rope(zq[:, sl], qg) * attn_scale).astype(BF16))

        zk = _dot(xn, w_ref[:, d_attn:2 * d_attn])
        kg = kg_ref[...]
        for c in range(d_attn // LANES):
            sl = slice(c * LANES, (c + 1) * LANES)
            kr = norm_rope(zk[:, sl], kg)
            store(1, sl, kr)
            store(2, sl, kr.astype(BF16))

        zv = _dot(xn, w_ref[:, 2 * d_attn:3 * d_attn])
        store(3, slice(None), zv)
        store(4, slice(None), zv.astype(BF16))

        o = 3 * d_attn
        store(5, slice(None), _dot(xn, w_ref[:, o:o + d_pool]))
        o += d_pool
        store(6, slice(None), _gelu(_dot(xn, w_ref[:, o:o + d_gmlp])))
        o += d_gmlp
        zg = _gelu(_dot(xn, w_ref[:, o:o + d_gmlp]))
        store(7, slice(None), _rms(zg, gg_ref[...]))

    def store_tile(idx, cols, val):
        out_refs[idx][:, cols] = val

    def store_both(idx, cols, val):
        out_refs[idx][:, cols] = val[:tm]
        out_refs[n_out + idx][:, cols] = val[tm:]

    i = pl.program_id(0)

    @pl.when(i == 0)
    def _():
        body(jnp.concatenate([x_ref[...], xs_ref[...]], axis=0),
             jnp.concatenate([cos_ref[...], coss_ref[...]], axis=0),
             jnp.concatenate([sin_ref[...], sins_ref[...]], axis=0), store_both)

    @pl.when(i > 0)
    def _():
        body(x_ref[...], cos_ref[...], sin_ref[...], store_tile)


def _proj(x, xs, g, w_in, qg, kg, gg, cos, sin, cos_s, sin_s, avg, layer, tm, table_tiles):
    m, d = x.shape
    ms = xs.shape[0]
    d_in = w_in.shape[-1]
    d_attn = N_HEADS_A * DV
    d_pool = 4 * LANES
    d_gmlp = N_HEADS_C * CHUNK
    row = lambda i: (i, 0)
    fixed = lambda i: (0, 0)
    tab = lambda i: (i % table_tiles, 0)
    par = lambda i: (layer, 0, 0)
    kern = functools.partial(_proj_kernel, tm=tm, d_attn=d_attn, d_pool=d_pool, d_gmlp=d_gmlp,
                             attn_scale=DK ** -0.5 * math.log2(math.e))
    widths = (d_attn, d_attn, d_attn, d_attn, d_attn, d_pool, d_gmlp, d_gmlp)
    dtypes = (BF16, F32, BF16, F32, BF16, F32, F32, F32)
    return pl.pallas_call(
        kern,
        grid=(m // tm,),
        in_specs=[
            pl.BlockSpec((tm, d), row),
            pl.BlockSpec((ms, d), fixed),
            pl.BlockSpec((None, 1, d), par),
            pl.BlockSpec((None, d, d_in), par),
            pl.BlockSpec((None, 1, LANES), par),
            pl.BlockSpec((None, 1, LANES), par),
            pl.BlockSpec((None, 1, d_gmlp), par),
            pl.BlockSpec((tm, LANES), tab),
            pl.BlockSpec((tm, LANES), tab),
            pl.BlockSpec((ms, LANES), fixed),
            pl.BlockSpec((ms, LANES), fixed),
            pl.BlockSpec((LANES, LANES), fixed),
        ],
        out_specs=[pl.BlockSpec((tm, w), row) for w in widths] + [pl.BlockSpec((ms, w), fixed) for w in widths],
        out_shape=[jax.ShapeDtypeStruct((m, w), dt) for w, dt in zip(widths, dtypes)]
        + [jax.ShapeDtypeStruct((ms, w), dt) for w, dt in zip(widths, dtypes)],
        compiler_params=pltpu.CompilerParams(
            dimension_semantics=("arbitrary",), vmem_limit_bytes=VMEM_LIMIT),
        name="proj",
    )(x, xs, g, w_in, qg, kg, gg, cos, sin, cos_s, sin_s, avg)


def _attn_kernel(lam_ref, q_ref, k_ref, v_ref, sub_ref, o_ref, *scratch, t, nh, l_init):
    qi = pl.program_id(2)
    per_head = [scratch[6 * hh:6 * hh + 6] for hh in range(nh)]

    @pl.when(qi == 0)
    def _():
        for hh, (vt_ref, *_) in enumerate(per_head):
            for jb in range(vt_ref.shape[0]):
                vt_ref[jb] = v_ref[jb * t:(jb + 1) * t, hh * DV:(hh + 1) * DV].astype(F32).T.astype(BF16)

    lane = lax.broadcasted_iota(jnp.int32, (t, LANES), 1)
    q2s = []
    for hh, (_, m_ref, l_ref, acc_ref, _, _) in enumerate(per_head):
        q = q_ref[:, hh * LANES:(hh + 1) * LANES]
        zero = jnp.zeros_like(q)
        q2s.append(jnp.concatenate([jnp.where(lane < DK, q, zero), jnp.where(lane >= DK, q, zero)], axis=0))
        m_ref[...] = jnp.full_like(m_ref, NEG)
        l_ref[...] = jnp.zeros_like(l_ref)
        acc_ref[...] = jnp.zeros_like(acc_ref)

    def scores(hh, j):
        kj = k_ref[pl.ds(pl.multiple_of(j * t, t), t), hh * LANES:(hh + 1) * LANES]
        return _dot_nt(kj, q2s[hh])

    def update(hh, j, s, masked):
        vt_ref, m_ref, l_ref, acc_ref, _, _ = per_head[hh]
        if masked:
            r = lax.broadcasted_iota(jnp.int32, s.shape, 0)
            c = lax.broadcasted_iota(jnp.int32, s.shape, 1)
            s = jnp.where(r <= jnp.where(c >= t, c - t, c), s, NEG)
        m_old = m_ref[...]
        m_new = jnp.maximum(m_old, jnp.max(s, axis=0, keepdims=True))
        p = jnp.exp2(s - m_new)
        alpha = jnp.exp2(m_old - m_new)
        l_ref[...] = alpha * l_ref[...] + jnp.sum(p, axis=0, keepdims=True)
        acc_ref[...] = alpha * acc_ref[...] + _dot(vt_ref[j], p.astype(BF16))
        m_ref[...] = m_new

    heads = range(nh)
    sa = [ph[4] for ph in per_head]
    sb = [ph[5] for ph in per_head]

    def body(i2, carry):
        j = 2 * i2
        for hh in heads:
            sb[hh][...] = scores(hh, j + 1)
        for hh in heads:
            update(hh, j, sa[hh][...], False)
        for hh in heads:
            sa[hh][...] = scores(hh, j + 2)
        for hh in heads:
            update(hh, j + 1, sb[hh][...], False)
        return carry

    for hh in heads:
        sa[hh][...] = scores(hh, 0)
    lax.fori_loop(0, qi // 2, body, 0)

    @pl.when(qi % 2 == 0)
    def _():
        for hh in heads:
            update(hh, qi, sa[hh][...], True)

    @pl.when(qi % 2 == 1)
    def _():
        for hh in heads:
            sb[hh][...] = scores(hh, qi)
        for hh in heads:
            update(hh, qi - 1, sa[hh][...], False)
        for hh in heads:
            update(hh, qi, sb[hh][...], True)

    lam = _lam(lam_ref, l_init)
    for hh, (_, _, l_ref, acc_ref, _, _) in enumerate(per_head):
        o = acc_ref[...] / l_ref[...]
        o = o[:, :t] - lam * o[:, t:]
        ms = jnp.mean(o * o, axis=0, keepdims=True)
        y = (o * lax.rsqrt(ms + EPS)).T
        o_ref[:, hh * DV:(hh + 1) * DV] = (y * sub_ref[...] * (1.0 - l_init)).astype(o_ref.dtype)


def _attn_prompt(lam, q, k, v, subln, layer, batch, seq, t, nh):
    nq = seq // t
    kern = functools.partial(_attn_kernel, t=t, nh=nh, l_init=_l_init(layer))
    head_scratch = [pltpu.VMEM((nq, DV, t), BF16),
                    pltpu.VMEM((1, 2 * t), F32), pltpu.VMEM((1, 2 * t), F32),
                    pltpu.VMEM((DV, 2 * t), F32),
                    pltpu.VMEM((t, 2 * t), F32), pltpu.VMEM((t, 2 * t), F32)]
    return pl.pallas_call(
        kern,
        grid=(batch, N_HEADS_A // nh, nq),
        in_specs=[
            pl.BlockSpec((None, 4, DK), lambda b, h, i: (layer, 0, 0)),
            pl.BlockSpec((t, nh * LANES), lambda b, h, i: (b * nq + i, h)),
            pl.BlockSpec((seq, nh * LANES), lambda b, h, i: (b, h)),
            pl.BlockSpec((seq, nh * LANES), lambda b, h, i: (b, h)),
            pl.BlockSpec((None, 1, DV), lambda b, h, i: (layer, 0, 0)),
        ],
        out_specs=pl.BlockSpec((t, nh * LANES), lambda b, h, i: (b * nq + i, h)),
        out_shape=jax.ShapeDtypeStruct((batch * seq, N_HEADS_A * DV), BF16),
        scratch_shapes=head_scratch * nh,
        compiler_params=pltpu.CompilerParams(
            dimension_semantics=("parallel", "parallel", "arbitrary"), vmem_limit_bytes=VMEM_LIMIT),
        name="attn_prompt",
    )(lam, q, k, v, subln)


def _dec_kernel(pt_ref, lam_ref, qm_ref, knew_ref, vnew_ref, sub_ref, *rest, g_pages, t_new, l_init):
    del pt_ref
    k_refs = rest[:g_pages]
    v_refs = rest[g_pages:2 * g_pages]
    o_ref, m_ref, l_ref, acc_ref = rest[2 * g_pages:]
    pg = pl.program_id(1)

    @pl.when(pg == 0)
    def _():
        m_ref[...] = jnp.full_like(m_ref, NEG)
        l_ref[...] = jnp.zeros_like(l_ref)
        acc_ref[...] = jnp.zeros_like(acc_ref)

    qm = qm_ref[...]

    def by_head(refs):
        return jnp.stack([
            jnp.concatenate([r[pl.ds(h, PAGE_SIZE, stride=N_HEADS_A), :] for r in refs], axis=0)
            for h in range(N_HEADS_A)]).astype(BF16)

    def update(s, v3):
        m_old = m_ref[...]
        m_new = jnp.maximum(m_old, jnp.max(s, axis=-1, keepdims=True))
        p = jnp.exp2(s - m_new)
        alpha = jnp.exp2(m_old - m_new)
        l_ref[...] = alpha * l_ref[...] + jnp.sum(p, axis=-1, keepdims=True)
        pv = jnp.einsum('hqk,hkd->hqd', p.astype(BF16), v3, preferred_element_type=F32)
        acc_ref[...] = alpha * acc_ref[...] + pv
        m_ref[...] = m_new

    def scores(k3):
        return jnp.einsum('hqd,hkd->hqk', qm, k3, preferred_element_type=F32)

    update(scores(by_head(k_refs)), by_head(v_refs))

    @pl.when(pg == pl.num_programs(1) - 1)
    def _():
        s_new = scores(knew_ref[...])
        r = lax.broadcasted_iota(jnp.int32, s_new.shape, 1)
        c = lax.broadcasted_iota(jnp.int32, s_new.shape, 2)
        update(jnp.where(c <= (r % t_new), s_new, NEG), vnew_ref[...])

        o = acc_ref[...] / l_ref[...]
        o = o[:, :t_new] - _lam(lam_ref, l_init) * o[:, t_new:]
        y = _rms(o, sub_ref[...]) * (1.0 - l_init)
        for h in range(N_HEADS_A):
            o_ref[:, h * DV:(h + 1) * DV] = y[h]


def _attn_sample(page_table, lam, qm, knew, vnew, subln, cache_k, cache_v, layer, g_pages, t_new):
    db, n_pages = page_table.shape
    width = N_HEADS_A * DV
    page_rows = PAGE_SIZE * N_HEADS_A
    kern = functools.partial(_dec_kernel, g_pages=g_pages, t_new=t_new, l_init=_l_init(layer))

    def page_spec(g):
        return pl.BlockSpec((None, None, page_rows, LANES),
                            lambda b, p, pt: (layer, pt[b, p * g_pages + g], 0, 0))

    kv_specs = [page_spec(g) for g in range(g_pages)]

    per_batch = lambda b, p, pt: (b, 0, 0, 0)
    grid_spec = pltpu.PrefetchScalarGridSpec(
        num_scalar_prefetch=1,
        grid=(db, n_pages // g_pages),
        in_specs=[
            pl.BlockSpec((None, 4, DK), lambda b, p, pt: (layer, 0, 0)),
            pl.BlockSpec((None, N_HEADS_A, 2 * t_new, LANES), per_batch),
            pl.BlockSpec((None, N_HEADS_A, PAGE_SIZE, LANES), per_batch),
            pl.BlockSpec((None, N_HEADS_A, PAGE_SIZE, LANES), per_batch),
            pl.BlockSpec((None, 1, DV), lambda b, p, pt: (layer, 0, 0)),
        ] + kv_specs + kv_specs,
        out_specs=pl.BlockSpec((t_new, width), lambda b, p, pt: (b, 0)),
        scratch_shapes=[pltpu.VMEM((N_HEADS_A, 2 * t_new, 1), F32),
                        pltpu.VMEM((N_HEADS_A, 2 * t_new, 1), F32),
                        pltpu.VMEM((N_HEADS_A, 2 * t_new, DV), F32)],
    )
    return pl.pallas_call(
        kern,
        grid_spec=grid_spec,
        out_shape=jax.ShapeDtypeStruct((db * t_new, width), F32),
        compiler_params=pltpu.CompilerParams(
            dimension_semantics=("parallel", "arbitrary"), vmem_limit_bytes=VMEM_LIMIT),
        name="attn_sample",
    )(page_table, lam, qm, knew, vnew, subln, *([cache_k] * len(kv_specs)), *([cache_v] * len(kv_specs)))


def _post_kernel(h_ref, a_ref, up_ref, halo_ref, ug_ref, vg_ref, pw_ref, ps_ref, ws_ref, gb_ref,
                 wo_ref, o_ref, ext_ref, cat_ref, *, tm, tiles_per_seq, d_attn, d_pool):
    i = pl.program_id(0)

    halo = halo_ref[...]
    if tiles_per_seq is not None:
        seq_tile = i % tiles_per_seq
        halo = jnp.where(seq_tile == 0, jnp.zeros_like(halo), halo)
        pos = seq_tile * tm + lax.broadcasted_iota(jnp.int32, (tm, LANES), 0)
    ext_ref[0:HALO, :] = halo
    ext_ref[HALO:HALO + tm, :] = up_ref[...]
    for g, w in enumerate(POOL_WINDOWS):
        sl = slice(g * LANES, (g + 1) * LANES)
        u = up_ref[:, sl]
        tot = u
        for k in range(1, w):
            tot = tot + ext_ref[HALO - k:HALO - k + tm, sl]
        if tiles_per_seq is not None:
            cnt = jnp.minimum(w, pos + 1).astype(F32)
            mean = tot / cnt
        else:
            mean = tot / float(w)
        y = _dot((mean - u).astype(BF16), pw_ref[g]) * ps_ref[:, sl]
        cat_ref[:, d_attn + g * LANES:d_attn + (g + 1) * LANES] = y.astype(BF16)

    r = lax.broadcasted_iota(jnp.int32, (CHUNK, CHUNK), 0)
    c = lax.broadcasted_iota(jnp.int32, (CHUNK, CHUNK), 1)
    o_g = d_attn + d_pool
    for hh in range(N_HEADS_C):
        wsc = jnp.where(r >= c, ws_ref[hh], 0.0).astype(BF16)
        bias = gb_ref[hh]
        cs = slice(hh * CHUNK, (hh + 1) * CHUNK)
        for n in range(tm // CHUNK):
            rs = slice(n * CHUNK, (n + 1) * CHUNK)
            gate = _dot(wsc, vg_ref[rs, cs].astype(BF16)) + bias
            cat_ref[rs, o_g + hh * CHUNK:o_g + (hh + 1) * CHUNK] = (ug_ref[rs, cs] * gate).astype(BF16)

    cat_ref[:, 0:d_attn] = a_ref[...]
    o_ref[...] = h_ref[...] + _dot(cat_ref[...], wo_ref[...])


def _post(h, a, up, halo_src, ug, vg, pool_w, pool_scale, ws, gb, w_out, layer, tm, tiles_per_seq):
    m, d = h.shape
    d_attn = a.shape[1]
    d_pool = up.shape[1]
    d_gmlp = ug.shape[1]
    row = lambda i: (i, 0)
    if tiles_per_seq is not None:
        halo_map = lambda i: (jnp.maximum(i * (tm // HALO) - 1, 0), 0)
    else:
        halo_map = row
    kern = functools.partial(_post_kernel, tm=tm, tiles_per_seq=tiles_per_seq, d_attn=d_attn,
                             d_pool=d_pool)
    return pl.pallas_call(
        kern,
        grid=(m // tm,),
        in_specs=[
            pl.BlockSpec((tm, d), row),
            pl.BlockSpec((tm, d_attn), row),
            pl.BlockSpec((tm, d_pool), row),
            pl.BlockSpec((HALO, d_pool), halo_map),
            pl.BlockSpec((tm, d_gmlp), row),
            pl.BlockSpec((tm, d_gmlp), row),
            pl.BlockSpec((None, len(POOL_WINDOWS), LANES, LANES), lambda i: (layer, 0, 0, 0)),
            pl.BlockSpec((None, 1, d_pool), lambda i: (layer, 0, 0)),
            pl.BlockSpec((None, N_HEADS_C, CHUNK, CHUNK), lambda i: (layer, 0, 0, 0)),
            pl.BlockSpec((None, N_HEADS_C, CHUNK, CHUNK), lambda i: (layer, 0, 0, 0)),
            pl.BlockSpec((None, d, d), lambda i: (layer, 0, 0)),
        ],
        out_specs=pl.BlockSpec((tm, d), row),
        out_shape=jax.ShapeDtypeStruct((m, d), F32),
        scratch_shapes=[pltpu.VMEM((HALO + tm, d_pool), F32), pltpu.VMEM((tm, d), BF16)],
        compiler_params=pltpu.CompilerParams(
            dimension_semantics=("parallel",), vmem_limit_bytes=VMEM_LIMIT),
        name="post",
    )(h, a, up, halo_src, ug, vg, pool_w, pool_scale, ws, gb, w_out)


def _rope_tables(pos):
    half = DK // 2
    inv_freq = jnp.power(ROPE_THETA, -jnp.arange(half, dtype=F32) / half)
    ang = pos.astype(F32)[:, None] * inv_freq[None, :]
    cos = jnp.tile(jnp.cos(ang), (1, LANES // half))
    sign = jnp.where((jnp.arange(LANES) % DK) < half, -1.0, 1.0).astype(F32)
    sin = jnp.tile(jnp.sin(ang), (1, LANES // half)) * sign[None, :]
    return cos, sin


def kernel(x_prompt, x_sample, cache_k, cache_v, state_pool, page_table, norm_ffn1, w1_gate, w1_up, w1_down, norm_mix, w_in, w_out, q_norm, k_norm, lambda_q1, lambda_k1, lambda_q2, lambda_k2, subln, pool_w, pool_scale, gmlp_norm, gmlp_ws, gmlp_b, norm_ffn2, w2_gate, w2_up, w2_down):
    b, seq, d = x_prompt.shape
    db, t_new, _ = x_sample.shape
    depth = w_in.shape[0]
    n_pages = page_table.shape[1]
    past = n_pages * PAGE_SIZE
    width = N_HEADS_A * DV
    d_pool = pool_scale.shape[1]
    d_gmlp = gmlp_norm.shape[1]
    ms = db * t_new

    tm_ffn, tf = 1024, 256
    tm_proj = 256
    t_attn = 512
    heads_per_step = 4
    tm_post = 256
    g_pages = 16

    bf = lambda w: w.astype(BF16)
    w1g, w1u, w1d = w1_gate, w1_up, w1_down
    w2g, w2u, w2d = w2_gate, w2_up, w2_down
    w_in_b, w_out_b, pool_w_b = bf(w_in), bf(w_out), bf(pool_w)
    row3 = lambda p: p.reshape(depth, 1, -1)
    n1, nm, n2 = row3(norm_ffn1), row3(norm_mix), row3(norm_ffn2)
    qg = row3(jnp.tile(q_norm, (1, LANES // DK)))
    kg = row3(jnp.tile(k_norm, (1, LANES // DK)))
    gg, sub, ps = row3(gmlp_norm), row3(subln), row3(pool_scale)
    lam = jnp.stack([lambda_q1, lambda_k1, lambda_q2, lambda_k2], axis=1)
    gb = jnp.broadcast_to(gmlp_b[:, :, :, None], gmlp_b.shape + (CHUNK,))
    grp = jnp.arange(LANES) // DK
    avg = jnp.where(grp[:, None] == grp[None, :], 1.0 / DK, 0.0).astype(BF16)

    cos_p, sin_p = _rope_tables(jnp.arange(seq, dtype=jnp.int32))
    cos_s, sin_s = _rope_tables(past + (jnp.arange(ms, dtype=jnp.int32) % t_new))


    ck = cache_k.reshape(depth, cache_k.shape[1], PAGE_SIZE * N_HEADS_A, LANES)
    cv = cache_v.reshape(depth, cache_v.shape[1], PAGE_SIZE * N_HEADS_A, LANES)
    new_pad = ((0, 0), (0, 0), (0, PAGE_SIZE - t_new), (0, 0))

    hp = x_prompt.reshape(b * seq, d)
    hs = x_sample.reshape(ms, d)
    kp_l, vp_l, poolp_l, ks_l, vs_l, pools_l, gv_l = [], [], [], [], [], [], []
    pad_rows = lambda x: jnp.pad(x.reshape(db, t_new, -1), ((0, 0), (0, CHUNK - t_new), (0, 0))
                                 ).reshape(db * CHUNK, -1)

    for l in range(depth):
        hp, hs = _ffn(hp, hs, n1, w1g, w1u, w1d, l, tm_ffn, tf)

        proj_out = _proj(hp, hs, nm, w_in_b, qg, kg, gg, cos_p, sin_p, cos_s, sin_s, avg, l,
                         tm_proj, seq // tm_proj)

        q, k, kb, v, vb, up, ug, vg = proj_out[:8]
        a = _attn_prompt(lam, q, kb, vb, sub, l, b, seq, t_attn, heads_per_step)
        hp = _post(hp, a, up, up, ug, vg, pool_w_b, ps, gmlp_ws, gb, w_out_b, l, tm_post,
                   seq // tm_post)
        kp_l.append(k)
        vp_l.append(v)
        poolp_l.append(up.reshape(b, seq, d_pool)[:, seq - POOL_BUF:])

        q, k, kb, v, vb, us, ug, vg = proj_out[8:]
        q5 = q.reshape(db, t_new, N_HEADS_A, 2, DK).transpose(0, 2, 1, 3, 4)
        comp = jnp.arange(2)
        qm = jnp.where(comp[None, None, :, None, None, None] == comp[None, None, None, None, :, None],
                       q5[:, :, None], jnp.zeros((), BF16))
        qm = qm.reshape(db, N_HEADS_A, 2 * t_new, LANES)
        knew = jnp.pad(kb.reshape(db, t_new, N_HEADS_A, LANES).transpose(0, 2, 1, 3), new_pad)
        vnew = jnp.pad(vb.reshape(db, t_new, N_HEADS_A, LANES).transpose(0, 2, 1, 3), new_pad)
        a = _attn_sample(page_table, lam, qm, knew, vnew, sub, ck, cv, l, g_pages, t_new)
        ext = jnp.concatenate([state_pool[l], us.reshape(db, t_new, d_pool)], axis=1)
        halo = jnp.pad(state_pool[l], ((0, 0), (HALO - POOL_BUF, 0), (0, 0))).reshape(db * HALO, d_pool)
        hs_pad = _post(pad_rows(hs), pad_rows(a.astype(BF16)), pad_rows(us), halo, pad_rows(ug),
                       pad_rows(vg), pool_w_b, ps, gmlp_ws, gb, w_out_b, l, CHUNK, None)
        hs = hs_pad.reshape(db, CHUNK, d)[:, :t_new].reshape(ms, d)
        ks_l.append(k)
        vs_l.append(v)
        pools_l.append(ext[:, t_new:])
        gv_l.append(vg)

        hp, hs = _ffn(hp, hs, n2, w2g, w2u, w2d, l, tm_ffn, tf)

    n_pg = seq // PAGE_SIZE
    return (
        hp.reshape(b, seq, d),
        hs.reshape(db, t_new, d),
        jnp.stack(kp_l).reshape(depth, b, n_pg, PAGE_SIZE, N_HEADS_A, 2 * DK),
        jnp.stack(vp_l).reshape(depth, b, n_pg, PAGE_SIZE, N_HEADS_A, DV),
        jnp.stack(poolp_l),
        jnp.stack(ks_l).reshape(depth, db, t_new, N_HEADS_A, 2 * DK),
        jnp.stack(vs_l).reshape(depth, db, t_new, N_HEADS_A, DV),
        jnp.stack(pools_l),
        jnp.stack(gv_l).reshape(depth, db, t_new, d_gmlp),
    )
```

```python
import functools
import math

import jax
import jax.numpy as jnp
from jax import lax
from jax.experimental import pallas as pl
from jax.experimental.pallas import tpu as pltpu

F32 = jnp.float32
BF16 = jnp.bfloat16

EPS = 1e-6
ROPE_THETA = 10000.0
N_HEADS_A = 8
DV = 128
DK = 64
POOL_WINDOWS = (2, 4, 8, 16)
POOL_BUF = 15
HALO = 16
CHUNK = 128
N_HEADS_C = 4
PAGE_SIZE = 128
LANES = 128
NEG = -1e30
ONES_ROWS = 16
VMEM_LIMIT = 56 * 1024 * 1024

_NT = (((1,), (1,)), ((), ()))


def _dot(a, b):
    return jnp.dot(a, b, preferred_element_type=F32)


def _dot_nt(a, b):
    return lax.dot_general(a, b, _NT, preferred_element_type=F32)


def _rms(x, g):
    ms = jnp.mean(x * x, axis=-1, keepdims=True)
    return x * lax.rsqrt(ms + EPS) * g


def _gelu(x):
    return 0.5 * x * (1.0 + lax.erf(x * (2.0 ** -0.5)))


def _l_init(layer):
    return 0.8 - 0.6 * math.exp(-0.3 * layer)


def _lam(lam_ref, l_init):
    lp = lam_ref[...]
    s1 = jnp.sum(lp[0:1] * lp[1:2], axis=-1, keepdims=True)
    s2 = jnp.sum(lp[2:3] * lp[3:4], axis=-1, keepdims=True)
    return jnp.exp(s1) - jnp.exp(s2) + l_init


def _ffn_kernel(x_ref, xs_ref, g_ref, wg_ref, wu_ref, wd_ref, o_ref, os_ref, xn_ref, *, tm):
    i = pl.program_id(0)
    j = pl.program_id(1)
    last_j = pl.num_programs(1) - 1

    def hidden(xn):
        gate = _dot(xn, wg_ref[...].astype(BF16))
        up = _dot(xn, wu_ref[...].astype(BF16))
        return (gate * jax.nn.sigmoid(gate) * up).astype(BF16)

    @pl.when(j == 0)
    def _():
        xn_ref[0:tm, :] = _rms(x_ref[...], g_ref[...]).astype(BF16)
        o_ref[...] = jnp.zeros_like(o_ref)

    @pl.when((j == 0) & (i == 0))
    def _():
        xn_ref[tm:, :] = _rms(xs_ref[...], g_ref[...]).astype(BF16)
        os_ref[...] = jnp.zeros_like(os_ref)

    @pl.when(i == 0)
    def _():
        act = hidden(xn_ref[...])
        wd = wd_ref[...].astype(BF16)
        o_ref[...] += _dot(act[:tm], wd)
        os_ref[...] += _dot(act[tm:], wd)

    @pl.when(i > 0)
    def _():
        o_ref[...] += _dot(hidden(xn_ref[0:tm, :]), wd_ref[...].astype(BF16))

    @pl.when(j == last_j)
    def _():
        o_ref[...] = x_ref[...] + 0.5 * o_ref[...]

    @pl.when((j == last_j) & (i == 0))
    def _():
        os_ref[...] = xs_ref[...] + 0.5 * os_ref[...]


def _ffn(x, xs, g, wg, wu, wd, layer, tm, tf):
    m, d = x.shape
    ms = xs.shape[0]
    f = wg.shape[-1]
    return pl.pallas_call(
        functools.partial(_ffn_kernel, tm=tm),
        grid=(m // tm, f // tf),
        in_specs=[
            pl.BlockSpec((tm, d), lambda i, j: (i, 0)),
            pl.BlockSpec((ms, d), lambda i, j: (0, 0)),
            pl.BlockSpec((None, 1, d), lambda i, j: (layer, 0, 0)),
            pl.BlockSpec((None, d, tf), lambda i, j: (layer, 0, j)),
            pl.BlockSpec((None, d, tf), lambda i, j: (layer, 0, j)),
            pl.BlockSpec((None, tf, d), lambda i, j: (layer, j, 0)),
        ],
        out_specs=[pl.BlockSpec((tm, d), lambda i, j: (i, 0)),
                   pl.BlockSpec((ms, d), lambda i, j: (0, 0))],
        out_shape=[jax.ShapeDtypeStruct((m, d), F32), jax.ShapeDtypeStruct((ms, d), F32)],
        scratch_shapes=[pltpu.VMEM((tm + ms, d), BF16)],
        compiler_params=pltpu.CompilerParams(
            dimension_semantics=("arbitrary", "arbitrary"), vmem_limit_bytes=VMEM_LIMIT),
        name="ffn",
    )(x, xs, g, wg, wu, wd)


def _proj_kernel(x_ref, xs_ref, g_ref, w_ref, qg_ref, kg_ref, gg_ref, cos_ref, sin_ref, coss_ref, sins_ref,
                 avg_ref, *out_refs, tm, d_attn, d_pool, d_gmlp, attn_scale):
    n_out = len(out_refs) // 2
    avg = avg_ref[...]

    def body(x, cos, sin, store):
        rows = x.shape[0]
        xn = _rms(x, g_ref[...]).astype(BF16)
        lane = lax.broadcasted_iota(jnp.int32, (rows, LANES), 1)
        first_half = (lane % DK) < (DK // 2)

        def norm_rope(z, gain):
            z2 = z * z
            hi = z2.astype(BF16)
            lo = (z2 - hi.astype(F32)).astype(BF16)
            ms = _dot(hi, avg) + _dot(lo, avg)
            y = z * lax.rsqrt(ms + EPS) * gain
            partner = jnp.where(first_half, pltpu.roll(y, LANES - DK // 2, 1), pltpu.roll(y, DK // 2, 1))
            return y * cos + partner * sin

        zq = _dot(xn, w_ref[:, 0:d_attn])
        qg = qg_ref[...]
        for c in range(d_attn // LANES):
            sl = slice(c * LANES, (c + 1) * LANES)
            store(0, sl, (norm_rope(zq[:, sl], qg) * attn_scale).astype(BF16))

        zk = _dot(xn, w_ref[:, d_attn:2 * d_attn])
        kg = kg_ref[...]
        for c in range(d_attn // LANES):
            sl = slice(c * LANES, (c + 1) * LANES)
            kr = norm_rope(zk[:, sl], kg)
            store(1, sl, kr)
            store(2, sl, kr.astype(BF16))

        zv = _dot(xn, w_ref[:, 2 * d_attn:3 * d_attn])
        store(3, slice(None), zv)
        store(4, slice(None), zv.astype(BF16))

        o = 3 * d_attn
        store(5, slice(None), _dot(xn, w_ref[:, o:o + d_pool]))
        o += d_pool
        store(6, slice(None), _gelu(_dot(xn, w_ref[:, o:o + d_gmlp])))
        o += d_gmlp
        zg = _gelu(_dot(xn, w_ref[:, o:o + d_gmlp]))
        store(7, slice(None), _rms(zg, gg_ref[...]))

    def store_tile(idx, cols, val):
        out_refs[idx][:, cols] = val

    def store_both(idx, cols, val):
        out_refs[idx][:, cols] = val[:tm]
        out_refs[n_out + idx][:, cols] = val[tm:]

    i = pl.program_id(0)

    @pl.when(i == 0)
    def _():
        body(jnp.concatenate([x_ref[...], xs_ref[...]], axis=0),
             jnp.concatenate([cos_ref[...], coss_ref[...]], axis=0),
             jnp.concatenate([sin_ref[...], sins_ref[...]], axis=0), store_both)

    @pl.when(i > 0)
    def _():
        body(x_ref[...], cos_ref[...], sin_ref[...], store_tile)


def _proj(x, xs, g, w_in, qg, kg, gg, cos, sin, cos_s, sin_s, avg, layer, tm, table_tiles):
    m, d = x.shape
    ms = xs.shape[0]
    d_in = w_in.shape[-1]
    d_attn = N_HEADS_A * DV
    d_pool = 4 * LANES
    d_gmlp = N_HEADS_C * CHUNK
    row = lambda i: (i, 0)
    fixed = lambda i: (0, 0)
    tab = lambda i: (i % table_tiles, 0)
    par = lambda i: (layer, 0, 0)
    kern = functools.partial(_proj_kernel, tm=tm, d_attn=d_attn, d_pool=d_pool, d_gmlp=d_gmlp,
                             attn_scale=DK ** -0.5 * math.log2(math.e))
    widths = (d_attn, d_attn, d_attn, d_attn, d_attn, d_pool, d_gmlp, d_gmlp)
    dtypes = (BF16, F32, BF16, F32, BF16, F32, F32, F32)
    return pl.pallas_call(
        kern,
        grid=(m // tm,),
        in_specs=[
            pl.BlockSpec((tm, d), row),
            pl.BlockSpec((ms, d), fixed),
            pl.BlockSpec((None, 1, d), par),
            pl.BlockSpec((None, d, d_in), par),
            pl.BlockSpec((None, 1, LANES), par),
            pl.BlockSpec((None, 1, LANES), par),
            pl.BlockSpec((None, 1, d_gmlp), par),
            pl.BlockSpec((tm, LANES), tab),
            pl.BlockSpec((tm, LANES), tab),
            pl.BlockSpec((ms, LANES), fixed),
            pl.BlockSpec((ms, LANES), fixed),
            pl.BlockSpec((LANES, LANES), fixed),
        ],
        out_specs=[pl.BlockSpec((tm, w), row) for w in widths] + [pl.BlockSpec((ms, w), fixed) for w in widths],
        out_shape=[jax.ShapeDtypeStruct((m, w), dt) for w, dt in zip(widths, dtypes)]
        + [jax.ShapeDtypeStruct((ms, w), dt) for w, dt in zip(widths, dtypes)],
        compiler_params=pltpu.CompilerParams(
            dimension_semantics=("arbitrary",), vmem_limit_bytes=VMEM_LIMIT),
        name="proj",
    )(x, xs, g, w_in, qg, kg, gg, cos, sin, cos_s, sin_s, avg)


def _attn_kernel(lam_ref, q_ref, k_ref, v_ref, sub_ref, o_ref, *scratch, t, nh, l_init):
    qi = pl.program_id(2)
    per_head = [scratch[5 * hh:5 * hh + 5] for hh in range(nh)]

    @pl.when(qi == 0)
    def _():
        for hh, (vt_ref, *_) in enumerate(per_head):
            for jb in range(vt_ref.shape[0]):
                vt_ref[jb, 0:DV, :] = v_ref[jb * t:(jb + 1) * t, hh * DV:(hh + 1) * DV].astype(F32).T.astype(BF16)
                vt_ref[jb, DV:, :] = jnp.ones((ONES_ROWS, t), BF16)

    lane = lax.broadcasted_iota(jnp.int32, (t, LANES), 1)
    q2s = []
    for hh, (_, m_ref, acc_ref, _, _) in enumerate(per_head):
        q = q_ref[:, hh * LANES:(hh + 1) * LANES]
        zero = jnp.zeros_like(q)
        q2s.append(jnp.concatenate([jnp.where(lane < DK, q, zero), jnp.where(lane >= DK, q, zero)], axis=0))
        m_ref[...] = jnp.full_like(m_ref, NEG)
        acc_ref[...] = jnp.zeros_like(acc_ref)

    def scores(hh, j):
        kj = k_ref[pl.ds(pl.multiple_of(j * t, t), t), hh * LANES:(hh + 1) * LANES]
        return _dot_nt(kj, q2s[hh])

    def update(hh, j, s, masked):
        vt_ref, m_ref, acc_ref, _, _ = per_head[hh]
        if masked:
            r = lax.broadcasted_iota(jnp.int32, s.shape, 0)
            c = lax.broadcasted_iota(jnp.int32, s.shape, 1)
            s = jnp.where(r <= jnp.where(c >= t, c - t, c), s, NEG)
        m_old = m_ref[...]
        m_new = jnp.maximum(m_old, jnp.max(s, axis=0, keepdims=True))
        p = jnp.exp2(s - m_new)
        alpha = jnp.exp2(m_old - m_new)
        acc_ref[...] = alpha * acc_ref[...] + _dot(vt_ref[j], p.astype(BF16))
        m_ref[...] = m_new

    heads = range(nh)
    sa = [ph[3] for ph in per_head]
    sb = [ph[4] for ph in per_head]

    def body(i2, carry):
        j = 2 * i2
        for hh in heads:
            sb[hh][...] = scores(hh, j + 1)
        for hh in heads:
            update(hh, j, sa[hh][...], False)
        for hh in heads:
            sa[hh][...] = scores(hh, j + 2)
        for hh in heads:
            update(hh, j + 1, sb[hh][...], False)
        return carry

    for hh in heads:
        sa[hh][...] = scores(hh, 0)
    lax.fori_loop(0, qi // 2, body, 0)

    @pl.when(qi % 2 == 0)
    def _():
        for hh in heads:
            update(hh, qi, sa[hh][...], True)

    @pl.when(qi % 2 == 1)
    def _():
        for hh in heads:
            sb[hh][...] = scores(hh, qi)
        for hh in heads:
            update(hh, qi - 1, sa[hh][...], False)
        for hh in heads:
            update(hh, qi, sb[hh][...], True)

    lam = _lam(lam_ref, l_init)
    for hh, (_, _, acc_ref, _, _) in enumerate(per_head):
        o = acc_ref[0:DV, :] / acc_ref[DV:DV + 1, :]
        o = o[:, :t] - lam * o[:, t:]
        ms = jnp.mean(o * o, axis=0, keepdims=True)
        y = (o * lax.rsqrt(ms + EPS)).T
        o_ref[:, hh * DV:(hh + 1) * DV] = (y * sub_ref[...] * (1.0 - l_init)).astype(o_ref.dtype)


def _attn_prompt(lam, q, k, v, subln, layer, batch, seq, t, nh):
    nq = seq // t
    kern = functools.partial(_attn_kernel, t=t, nh=nh, l_init=_l_init(layer))
    head_scratch = [pltpu.VMEM((nq, DV + ONES_ROWS, t), BF16),
                    pltpu.VMEM((1, 2 * t), F32),
                    pltpu.VMEM((DV + ONES_ROWS, 2 * t), F32),
                    pltpu.VMEM((t, 2 * t), F32), pltpu.VMEM((t, 2 * t), F32)]
    return pl.pallas_call(
        kern,
        grid=(batch, N_HEADS_A // nh, nq),
        in_specs=[
            pl.BlockSpec((None, 4, DK), lambda b, h, i: (layer, 0, 0)),
            pl.BlockSpec((t, nh * LANES), lambda b, h, i: (b * nq + i, h)),
            pl.BlockSpec((seq, nh * LANES), lambda b, h, i: (b, h)),
            pl.BlockSpec((seq, nh * LANES), lambda b, h, i: (b, h)),
            pl.BlockSpec((None, 1, DV), lambda b, h, i: (layer, 0, 0)),
        ],
        out_specs=pl.BlockSpec((t, nh * LANES), lambda b, h, i: (b * nq + i, h)),
        out_shape=jax.ShapeDtypeStruct((batch * seq, N_HEADS_A * DV), BF16),
        scratch_shapes=head_scratch * nh,
        compiler_params=pltpu.CompilerParams(
            dimension_semantics=("parallel", "parallel", "arbitrary"), vmem_limit_bytes=VMEM_LIMIT),
        name="attn_prompt",
    )(lam, q, k, v, subln)


def _dec_kernel(pt_ref, lam_ref, qm_ref, knew_ref, vnew_ref, sub_ref, *rest, g_pages, t_new, l_init):
    del pt_ref
    k_refs = rest[:g_pages]
    v_refs = rest[g_pages:2 * g_pages]
    o_ref, m_ref, l_ref, acc_ref = rest[2 * g_pages:]
    pg = pl.program_id(1)

    @pl.when(pg == 0)
    def _():
        m_ref[...] = jnp.full_like(m_ref, NEG)
        l_ref[...] = jnp.zeros_like(l_ref)
        acc_ref[...] = jnp.zeros_like(acc_ref)

    qm = qm_ref[...]

    def by_head(refs):
        return jnp.stack([
            jnp.concatenate([r[pl.ds(h, PAGE_SIZE, stride=N_HEADS_A), :] for r in refs], axis=0)
            for h in range(N_HEADS_A)]).astype(BF16)

    def update(s, v3):
        m_old = m_ref[...]
        m_new = jnp.maximum(m_old, jnp.max(s, axis=-1, keepdims=True))
        p = jnp.exp2(s - m_new)
        alpha = jnp.exp2(m_old - m_new)
        l_ref[...] = alpha * l_ref[...] + jnp.sum(p, axis=-1, keepdims=True)
        pv = jnp.einsum('hqk,hkd->hqd', p.astype(BF16), v3, preferred_element_type=F32)
        acc_ref[...] = alpha * acc_ref[...] + pv
        m_ref[...] = m_new

    def scores(k3):
        return jnp.einsum('hqd,hkd->hqk', qm, k3, preferred_element_type=F32)

    update(scores(by_head(k_refs)), by_head(v_refs))

    @pl.when(pg == pl.num_programs(1) - 1)
    def _():
        s_new = scores(knew_ref[...])
        r = lax.broadcasted_iota(jnp.int32, s_new.shape, 1)
        c = lax.broadcasted_iota(jnp.int32, s_new.shape, 2)
        update(jnp.where(c <= (r % t_new), s_new, NEG), vnew_ref[...])

        o = acc_ref[...] / l_ref[...]
        o = o[:, :t_new] - _lam(lam_ref, l_init) * o[:, t_new:]
        y = _rms(o, sub_ref[...]) * (1.0 - l_init)
        for h in range(N_HEADS_A):
            o_ref[:, h * DV:(h + 1) * DV] = y[h]


def _attn_sample(page_table, lam, qm, knew, vnew, subln, cache_k, cache_v, layer, g_pages, t_new):
    db, n_pages = page_table.shape
    width = N_HEADS_A * DV
    page_rows = PAGE_SIZE * N_HEADS_A
    kern = functools.partial(_dec_kernel, g_pages=g_pages, t_new=t_new, l_init=_l_init(layer))

    def page_spec(g):
        return pl.BlockSpec((None, None, page_rows, LANES),
                            lambda b, p, pt: (layer, pt[b, p * g_pages + g], 0, 0))

    kv_specs = [page_spec(g) for g in range(g_pages)]

    per_batch = lambda b, p, pt: (b, 0, 0, 0)
    grid_spec = pltpu.PrefetchScalarGridSpec(
        num_scalar_prefetch=1,
        grid=(db, n_pages // g_pages),
        in_specs=[
            pl.BlockSpec((None, 4, DK), lambda b, p, pt: (layer, 0, 0)),
            pl.BlockSpec((None, N_HEADS_A, 2 * t_new, LANES), per_batch),
            pl.BlockSpec((None, N_HEADS_A, PAGE_SIZE, LANES), per_batch),
            pl.BlockSpec((None, N_HEADS_A, PAGE_SIZE, LANES), per_batch),
            pl.BlockSpec((None, 1, DV), lambda b, p, pt: (layer, 0, 0)),
        ] + kv_specs + kv_specs,
        out_specs=pl.BlockSpec((t_new, width), lambda b, p, pt: (b, 0)),
        scratch_shapes=[pltpu.VMEM((N_HEADS_A, 2 * t_new, 1), F32),
                        pltpu.VMEM((N_HEADS_A, 2 * t_new, 1), F32),
                        pltpu.VMEM((N_HEADS_A, 2 * t_new, DV), F32)],
    )
    return pl.pallas_call(
        kern,
        grid_spec=grid_spec,
        out_shape=jax.ShapeDtypeStruct((db * t_new, width), F32),
        compiler_params=pltpu.CompilerParams(
            dimension_semantics=("parallel", "arbitrary"), vmem_limit_bytes=VMEM_LIMIT),
        name="attn_sample",
    )(page_table, lam, qm, knew, vnew, subln, *([cache_k] * len(kv_specs)), *([cache_v] * len(kv_specs)))


def _post_kernel(h_ref, a_ref, up_ref, halo_ref, ug_ref, vg_ref, pw_ref, ps_ref, ws_ref, gb_ref,
                 wo_ref, o_ref, ext_ref, cat_ref, *, tm, tiles_per_seq, d_attn, d_pool):
    i = pl.program_id(0)

    halo = halo_ref[...]
    if tiles_per_seq is not None:
        seq_tile = i % tiles_per_seq
        halo = jnp.where(seq_tile == 0, jnp.zeros_like(halo), halo)
        pos = seq_tile * tm + lax.broadcasted_iota(jnp.int32, (tm, LANES), 0)
    ext_ref[0:HALO, :] = halo
    ext_ref[HALO:HALO + tm, :] = up_ref[...]
    for g, w in enumerate(POOL_WINDOWS):
        sl = slice(g * LANES, (g + 1) * LANES)
        u = up_ref[:, sl]
        tot = u
        for k in range(1, w):
            tot = tot + ext_ref[HALO - k:HALO - k + tm, sl]
        if tiles_per_seq is not None:
            cnt = jnp.minimum(w, pos + 1).astype(F32)
            mean = tot / cnt
        else:
            mean = tot / float(w)
        y = _dot((mean - u).astype(BF16), pw_ref[g]) * ps_ref[:, sl]
        cat_ref[:, d_attn + g * LANES:d_attn + (g + 1) * LANES] = y.astype(BF16)

    r = lax.broadcasted_iota(jnp.int32, (CHUNK, CHUNK), 0)
    c = lax.broadcasted_iota(jnp.int32, (CHUNK, CHUNK), 1)
    o_g = d_attn + d_pool
    for hh in range(N_HEADS_C):
        wsc = jnp.where(r >= c, ws_ref[hh], 0.0).astype(BF16)
        bias = gb_ref[hh]
        cs = slice(hh * CHUNK, (hh + 1) * CHUNK)
        for n in range(tm // CHUNK):
            rs = slice(n * CHUNK, (n + 1) * CHUNK)
            gate = _dot(wsc, vg_ref[rs, cs].astype(BF16)) + bias
            cat_ref[rs, o_g + hh * CHUNK:o_g + (hh + 1) * CHUNK] = (ug_ref[rs, cs] * gate).astype(BF16)

    cat_ref[:, 0:d_attn] = a_ref[...]
    o_ref[...] = h_ref[...] + _dot(cat_ref[...], wo_ref[...])


def _post(h, a, up, halo_src, ug, vg, pool_w, pool_scale, ws, gb, w_out, layer, tm, tiles_per_seq):
    m, d = h.shape
    d_attn = a.shape[1]
    d_pool = up.shape[1]
    d_gmlp = ug.shape[1]
    row = lambda i: (i, 0)
    if tiles_per_seq is not None:
        halo_map = lambda i: (jnp.maximum(i * (tm // HALO) - 1, 0), 0)
    else:
        halo_map = row
    kern = functools.partial(_post_kernel, tm=tm, tiles_per_seq=tiles_per_seq, d_attn=d_attn,
                             d_pool=d_pool)
    return pl.pallas_call(
        kern,
        grid=(m // tm,),
        in_specs=[
            pl.BlockSpec((tm, d), row),
            pl.BlockSpec((tm, d_attn), row),
            pl.BlockSpec((tm, d_pool), row),
            pl.BlockSpec((HALO, d_pool), halo_map),
            pl.BlockSpec((tm, d_gmlp), row),
            pl.BlockSpec((tm, d_gmlp), row),
            pl.BlockSpec((None, len(POOL_WINDOWS), LANES, LANES), lambda i: (layer, 0, 0, 0)),
            pl.BlockSpec((None, 1, d_pool), lambda i: (layer, 0, 0)),
            pl.BlockSpec((None, N_HEADS_C, CHUNK, CHUNK), lambda i: (layer, 0, 0, 0)),
            pl.BlockSpec((None, N_HEADS_C, CHUNK, CHUNK), lambda i: (layer, 0, 0, 0)),
            pl.BlockSpec((None, d, d), lambda i: (layer, 0, 0)),
        ],
        out_specs=pl.BlockSpec((tm, d), row),
        out_shape=jax.ShapeDtypeStruct((m, d), F32),
        scratch_shapes=[pltpu.VMEM((HALO + tm, d_pool), F32), pltpu.VMEM((tm, d), BF16)],
        compiler_params=pltpu.CompilerParams(
            dimension_semantics=("parallel",), vmem_limit_bytes=VMEM_LIMIT),
        name="post",
    )(h, a, up, halo_src, ug, vg, pool_w, pool_scale, ws, gb, w_out)


def _rope_tables(pos):
    half = DK // 2
    inv_freq = jnp.power(ROPE_THETA, -jnp.arange(half, dtype=F32) / half)
    ang = pos.astype(F32)[:, None] * inv_freq[None, :]
    cos = jnp.tile(jnp.cos(ang), (1, LANES // half))
    sign = jnp.where((jnp.arange(LANES) % DK) < half, -1.0, 1.0).astype(F32)
    sin = jnp.tile(jnp.sin(ang), (1, LANES // half)) * sign[None, :]
    return cos, sin


def kernel(x_prompt, x_sample, cache_k, cache_v, state_pool, page_table, norm_ffn1, w1_gate, w1_up, w1_down, norm_mix, w_in, w_out, q_norm, k_norm, lambda_q1, lambda_k1, lambda_q2, lambda_k2, subln, pool_w, pool_scale, gmlp_norm, gmlp_ws, gmlp_b, norm_ffn2, w2_gate, w2_up, w2_down):
    b, seq, d = x_prompt.shape
    db, t_new, _ = x_sample.shape
    depth = w_in.shape[0]
    n_pages = page_table.shape[1]
    past = n_pages * PAGE_SIZE
    width = N_HEADS_A * DV
    d_pool = pool_scale.shape[1]
    d_gmlp = gmlp_norm.shape[1]
    ms = db * t_new

    tm_ffn, tf = 1024, 256
    tm_proj = 256
    t_attn = 512
    heads_per_step = 4
    tm_post = 256
    g_pages = 16

    bf = lambda w: w.astype(BF16)
    w1g, w1u, w1d = w1_gate, w1_up, w1_down
    w2g, w2u, w2d = w2_gate, w2_up, w2_down
    w_in_b, w_out_b, pool_w_b = bf(w_in), bf(w_out), bf(pool_w)
    row3 = lambda p: p.reshape(depth, 1, -1)
    n1, nm, n2 = row3(norm_ffn1), row3(norm_mix), row3(norm_ffn2)
    qg = row3(jnp.tile(q_norm, (1, LANES // DK)))
    kg = row3(jnp.tile(k_norm, (1, LANES // DK)))
    gg, sub, ps = row3(gmlp_norm), row3(subln), row3(pool_scale)
    lam = jnp.stack([lambda_q1, lambda_k1, lambda_q2, lambda_k2], axis=1)
    gb = jnp.broadcast_to(gmlp_b[:, :, :, None], gmlp_b.shape + (CHUNK,))
    grp = jnp.arange(LANES) // DK
    avg = jnp.where(grp[:, None] == grp[None, :], 1.0 / DK, 0.0).astype(BF16)

    cos_p, sin_p = _rope_tables(jnp.arange(seq, dtype=jnp.int32))
    cos_s, sin_s = _rope_tables(past + (jnp.arange(ms, dtype=jnp.int32) % t_new))


    ck = cache_k.reshape(depth, cache_k.shape[1], PAGE_SIZE * N_HEADS_A, LANES)
    cv = cache_v.reshape(depth, cache_v.shape[1], PAGE_SIZE * N_HEADS_A, LANES)
    new_pad = ((0, 0), (0, 0), (0, PAGE_SIZE - t_new), (0, 0))

    hp = x_prompt.reshape(b * seq, d)
    hs = x_sample.reshape(ms, d)
    kp_l, vp_l, poolp_l, ks_l, vs_l, pools_l, gv_l = [], [], [], [], [], [], []
    pad_rows = lambda x: jnp.pad(x.reshape(db, t_new, -1), ((0, 0), (0, CHUNK - t_new), (0, 0))
                                 ).reshape(db * CHUNK, -1)

    for l in range(depth):
        hp, hs = _ffn(hp, hs, n1, w1g, w1u, w1d, l, tm_ffn, tf)

        proj_out = _proj(hp, hs, nm, w_in_b, qg, kg, gg, cos_p, sin_p, cos_s, sin_s, avg, l,
                         tm_proj, seq // tm_proj)

        q, k, kb, v, vb, up, ug, vg = proj_out[:8]
        a = _attn_prompt(lam, q, kb, vb, sub, l, b, seq, t_attn, heads_per_step)
        hp = _post(hp, a, up, up, ug, vg, pool_w_b, ps, gmlp_ws, gb, w_out_b, l, tm_post,
                   seq // tm_post)
        kp_l.append(k)
        vp_l.append(v)
        poolp_l.append(up.reshape(b, seq, d_pool)[:, seq - POOL_BUF:])

        q, k, kb, v, vb, us, ug, vg = proj_out[8:]
        q5 = q.reshape(db, t_new, N_HEADS_A, 2, DK).transpose(0, 2, 1, 3, 4)
        comp = jnp.arange(2)
        qm = jnp.where(comp[None, None, :, None, None, None] == comp[None, None, None, None, :, None],
                       q5[:, :, None], jnp.zeros((), BF16))
        qm = qm.reshape(db, N_HEADS_A, 2 * t_new, LANES)
        knew = jnp.pad(kb.reshape(db, t_new, N_HEADS_A, LANES).transpose(0, 2, 1, 3), new_pad)
        vnew = jnp.pad(vb.reshape(db, t_new, N_HEADS_A, LANES).transpose(0, 2, 1, 3), new_pad)
        a = _attn_sample(page_table, lam, qm, knew, vnew, sub, ck, cv, l, g_pages, t_new)
        ext = jnp.concatenate([state_pool[l], us.reshape(db, t_new, d_pool)], axis=1)
        halo = jnp.pad(state_pool[l], ((0, 0), (HALO - POOL_BUF, 0), (0, 0))).reshape(db * HALO, d_pool)
        hs_pad = _post(pad_rows(hs), pad_rows(a.astype(BF16)), pad_rows(us), halo, pad_rows(ug),
                       pad_rows(vg), pool_w_b, ps, gmlp_ws, gb, w_out_b, l, CHUNK, None)
        hs = hs_pad.reshape(db, CHUNK, d)[:, :t_new].reshape(ms, d)
        ks_l.append(k)
        vs_l.append(v)
        pools_l.append(ext[:, t_new:])
        gv_l.append(vg)

        hp, hs = _ffn(hp, hs, n2, w2g, w2u, w2d, l, tm_ffn, tf)

    n_pg = seq // PAGE_SIZE
    return (
        hp.reshape(b, seq, d),
        hs.reshape(db, t_new, d),
        jnp.stack(kp_l).reshape(depth, b, n_pg, PAGE_SIZE, N_HEADS_A, 2 * DK),
        jnp.stack(vp_l).reshape(depth, b, n_pg, PAGE_SIZE, N_HEADS_A, DV),
        jnp.stack(poolp_l),
        jnp.stack(ks_l).reshape(depth, db, t_new, N_HEADS_A, 2 * DK),
        jnp.stack(vs_l).reshape(depth, db, t_new, N_HEADS_A, DV),
        jnp.stack(pools_l),
        jnp.stack(gv_l).reshape(depth, db, t_new, d_gmlp),
    )
```

```python
import functools
import math

import jax
import jax.numpy as jnp
from jax import lax
from jax.experimental import pallas as pl
from jax.experimental.pallas import tpu as pltpu

F32 = jnp.float32
BF16 = jnp.bfloat16

EPS = 1e-6
ROPE_THETA = 10000.0
N_HEADS_A = 8
DV = 128
DK = 64
POOL_WINDOWS = (2, 4, 8, 16)
POOL_BUF = 15
HALO = 16
CHUNK = 128
N_HEADS_C = 4
PAGE_SIZE = 128
LANES = 128
NEG = -1e30
ONES_ROWS = 16
VMEM_LIMIT = 56 * 1024 * 1024
FFN_VMEM_LIMIT = 63 * 1024 * 1024

_NT = (((1,), (1,)), ((), ()))


def _dot(a, b):
    return jnp.dot(a, b, preferred_element_type=F32)


def _dot_nt(a, b):
    return lax.dot_general(a, b, _NT, preferred_element_type=F32)


def _rms(x, g):
    ms = jnp.mean(x * x, axis=-1, keepdims=True)
    return x * lax.rsqrt(ms + EPS) * g


def _gelu(x):
    return 0.5 * x * (1.0 + lax.erf(x * (2.0 ** -0.5)))


def _l_init(layer):
    return 0.8 - 0.6 * math.exp(-0.3 * layer)


def _lam(lam_ref, l_init):
    lp = lam_ref[...]
    s1 = jnp.sum(lp[0:1] * lp[1:2], axis=-1, keepdims=True)
    s2 = jnp.sum(lp[2:3] * lp[3:4], axis=-1, keepdims=True)
    return jnp.exp(s1) - jnp.exp(s2) + l_init


def _ffn_kernel(x_ref, xs_ref, g_ref, wg_ref, wu_ref, wd_ref, o_ref, os_ref, xn_ref, *, tm):
    i = pl.program_id(0)
    j = pl.program_id(1)
    last_j = pl.num_programs(1) - 1

    def hidden(xn):
        gate = _dot(xn, wg_ref[...].astype(BF16))
        up = _dot(xn, wu_ref[...].astype(BF16))
        return (gate * jax.nn.sigmoid(gate) * up).astype(BF16)

    @pl.when(j == 0)
    def _():
        xn_ref[0:tm, :] = _rms(x_ref[...], g_ref[...]).astype(BF16)
        o_ref[...] = jnp.zeros_like(o_ref)

    @pl.when((j == 0) & (i == 0))
    def _():
        xn_ref[tm:, :] = _rms(xs_ref[...], g_ref[...]).astype(BF16)
        os_ref[...] = jnp.zeros_like(os_ref)

    @pl.when(i == 0)
    def _():
        act = hidden(xn_ref[...])
        wd = wd_ref[...].astype(BF16)
        o_ref[...] += _dot(act[:tm], wd)
        os_ref[...] += _dot(act[tm:], wd)

    @pl.when(i > 0)
    def _():
        o_ref[...] += _dot(hidden(xn_ref[0:tm, :]), wd_ref[...].astype(BF16))

    @pl.when(j == last_j)
    def _():
        o_ref[...] = x_ref[...] + 0.5 * o_ref[...]

    @pl.when((j == last_j) & (i == 0))
    def _():
        os_ref[...] = xs_ref[...] + 0.5 * os_ref[...]


def _ffn(x, xs, g, wg, wu, wd, layer, tm, tf):
    m, d = x.shape
    ms = xs.shape[0]
    f = wg.shape[-1]
    return pl.pallas_call(
        functools.partial(_ffn_kernel, tm=tm),
        grid=(m // tm, f // tf),
        in_specs=[
            pl.BlockSpec((tm, d), lambda i, j: (i, 0), pipeline_mode=pl.Buffered(1)),
            pl.BlockSpec((ms, d), lambda i, j: (0, 0)),
            pl.BlockSpec((None, 1, d), lambda i, j: (layer, 0, 0)),
            pl.BlockSpec((None, d, tf), lambda i, j: (layer, 0, j)),
            pl.BlockSpec((None, d, tf), lambda i, j: (layer, 0, j)),
            pl.BlockSpec((None, tf, d), lambda i, j: (layer, j, 0)),
        ],
        out_specs=[pl.BlockSpec((tm, d), lambda i, j: (i, 0)),
                   pl.BlockSpec((ms, d), lambda i, j: (0, 0))],
        out_shape=[jax.ShapeDtypeStruct((m, d), F32), jax.ShapeDtypeStruct((ms, d), F32)],
        scratch_shapes=[pltpu.VMEM((tm + ms, d), BF16)],
        compiler_params=pltpu.CompilerParams(
            dimension_semantics=("arbitrary", "arbitrary"), vmem_limit_bytes=FFN_VMEM_LIMIT),
        name="ffn",
    )(x, xs, g, wg, wu, wd)


def _proj_kernel(x_ref, xs_ref, g_ref, w_ref, qg_ref, kg_ref, gg_ref, cos_ref, sin_ref, coss_ref, sins_ref,
                 avg_ref, *out_refs, tm, d_attn, d_pool, d_gmlp, attn_scale):
    n_out = len(out_refs) // 2
    avg = avg_ref[...]

    def body(x, cos, sin, store):
        rows = x.shape[0]
        xn = _rms(x, g_ref[...]).astype(BF16)
        lane = lax.broadcasted_iota(jnp.int32, (rows, LANES), 1)
        first_half = (lane % DK) < (DK // 2)

        def norm_rope(z, gain):
            z2 = z * z
            hi = z2.astype(BF16)
            lo = (z2 - hi.astype(F32)).astype(BF16)
            ms = _dot(hi, avg) + _dot(lo, avg)
            y = z * lax.rsqrt(ms + EPS) * gain
            partner = jnp.where(first_half, pltpu.roll(y, LANES - DK // 2, 1), pltpu.roll(y, DK // 2, 1))
            return y * cos + partner * sin

        zq = _dot(xn, w_ref[:, 0:d_attn])
        qg = qg_ref[...]
        for c in range(d_attn // LANES):
            sl = slice(c * LANES, (c + 1) * LANES)
            store(0, sl, (norm_rope(zq[:, sl], qg) * attn_scale).astype(BF16))

        zk = _dot(xn, w_ref[:, d_attn:2 * d_attn])
        kg = kg_ref[...]
        for c in range(d_attn // LANES):
            sl = slice(c * LANES, (c + 1) * LANES)
            kr = norm_rope(zk[:, sl], kg)
            store(1, sl, kr)
            store(2, sl, kr.astype(BF16))

        zv = _dot(xn, w_ref[:, 2 * d_attn:3 * d_attn])
        store(3, slice(None), zv)
        store(4, slice(None), zv.astype(BF16))

        o = 3 * d_attn
        store(5, slice(None), _dot(xn, w_ref[:, o:o + d_pool]))
        o += d_pool
        store(6, slice(None), _gelu(_dot(xn, w_ref[:, o:o + d_gmlp])))
        o += d_gmlp
        zg = _gelu(_dot(xn, w_ref[:, o:o + d_gmlp]))
        store(7, slice(None), _rms(zg, gg_ref[...]))

    def store_tile(idx, cols, val):
        out_refs[idx][:, cols] = val

    def store_both(idx, cols, val):
        out_refs[idx][:, cols] = val[:tm]
        out_refs[n_out + idx][:, cols] = val[tm:]

    i = pl.program_id(0)

    @pl.when(i == 0)
    def _():
        body(jnp.concatenate([x_ref[...], xs_ref[...]], axis=0),
             jnp.concatenate([cos_ref[...], coss_ref[...]], axis=0),
             jnp.concatenate([sin_ref[...], sins_ref[...]], axis=0), store_both)

    @pl.when(i > 0)
    def _():
        body(x_ref[...], cos_ref[...], sin_ref[...], store_tile)


def _proj(x, xs, g, w_in, qg, kg, gg, cos, sin, cos_s, sin_s, avg, layer, tm, table_tiles):
    m, d = x.shape
    ms = xs.shape[0]
    d_in = w_in.shape[-1]
    d_attn = N_HEADS_A * DV
    d_pool = 4 * LANES
    d_gmlp = N_HEADS_C * CHUNK
    row = lambda i: (i, 0)
    fixed = lambda i: (0, 0)
    tab = lambda i: (i % table_tiles, 0)
    par = lambda i: (layer, 0, 0)
    kern = functools.partial(_proj_kernel, tm=tm, d_attn=d_attn, d_pool=d_pool, d_gmlp=d_gmlp,
                             attn_scale=DK ** -0.5 * math.log2(math.e))
    widths = (d_attn, d_attn, d_attn, d_attn, d_attn, d_pool, d_gmlp, d_gmlp)
    dtypes = (BF16, F32, BF16, F32, BF16, F32, F32, F32)
    return pl.pallas_call(
        kern,
        grid=(m // tm,),
        in_specs=[
            pl.BlockSpec((tm, d), row),
            pl.BlockSpec((ms, d), fixed),
            pl.BlockSpec((None, 1, d), par),
            pl.BlockSpec((None, d, d_in), par),
            pl.BlockSpec((None, 1, LANES), par),
            pl.BlockSpec((None, 1, LANES), par),
            pl.BlockSpec((None, 1, d_gmlp), par),
            pl.BlockSpec((tm, LANES), tab),
            pl.BlockSpec((tm, LANES), tab),
            pl.BlockSpec((ms, LANES), fixed),
            pl.BlockSpec((ms, LANES), fixed),
            pl.BlockSpec((LANES, LANES), fixed),
        ],
        out_specs=[pl.BlockSpec((tm, w), row) for w in widths] + [pl.BlockSpec((ms, w), fixed) for w in widths],
        out_shape=[jax.ShapeDtypeStruct((m, w), dt) for w, dt in zip(widths, dtypes)]
        + [jax.ShapeDtypeStruct((ms, w), dt) for w, dt in zip(widths, dtypes)],
        compiler_params=pltpu.CompilerParams(
            dimension_semantics=("arbitrary",), vmem_limit_bytes=VMEM_LIMIT),
        name="proj",
    )(x, xs, g, w_in, qg, kg, gg, cos, sin, cos_s, sin_s, avg)


def _attn_kernel(lam_ref, q_ref, k_ref, v_ref, sub_ref, o_ref, *scratch, t, nh, l_init):
    qi = pl.program_id(2)
    per_head = [scratch[5 * hh:5 * hh + 5] for hh in range(nh)]

    @pl.when(qi == 0)
    def _():
        for hh, (vt_ref, *_) in enumerate(per_head):
            for jb in range(vt_ref.shape[0]):
                vt_ref[jb, 0:DV, :] = v_ref[jb * t:(jb + 1) * t, hh * DV:(hh + 1) * DV].astype(F32).T.astype(BF16)
                vt_ref[jb, DV:, :] = jnp.ones((ONES_ROWS, t), BF16)

    lane = lax.broadcasted_iota(jnp.int32, (t, LANES), 1)
    q2s = []
    for hh, (_, m_ref, acc_ref, _, _) in enumerate(per_head):
        q = q_ref[:, hh * LANES:(hh + 1) * LANES]
        zero = jnp.zeros_like(q)
        q2s.append(jnp.concatenate([jnp.where(lane < DK, q, zero), jnp.where(lane >= DK, q, zero)], axis=0))
        m_ref[...] = jnp.full_like(m_ref, NEG)
        acc_ref[...] = jnp.zeros_like(acc_ref)

    def scores(hh, j):
        kj = k_ref[pl.ds(pl.multiple_of(j * t, t), t), hh * LANES:(hh + 1) * LANES]
        return _dot_nt(kj, q2s[hh])

    def update(hh, j, s, masked):
        vt_ref, m_ref, acc_ref, _, _ = per_head[hh]
        if masked:
            r = lax.broadcasted_iota(jnp.int32, s.shape, 0)
            c = lax.broadcasted_iota(jnp.int32, s.shape, 1)
            s = jnp.where(r <= jnp.where(c >= t, c - t, c), s, NEG)
        m_old = m_ref[...]
        m_new = jnp.maximum(m_old, jnp.max(s, axis=0, keepdims=True))
        p = jnp.exp2(s - m_new)
        alpha = jnp.exp2(m_old - m_new)
        acc_ref[...] = alpha * acc_ref[...] + _dot(vt_ref[j], p.astype(BF16))
        m_ref[...] = m_new

    heads = range(nh)
    sa = [ph[3] for ph in per_head]
    sb = [ph[4] for ph in per_head]

    def body(i2, carry):
        j = 2 * i2
        for hh in heads:
            sb[hh][...] = scores(hh, j + 1)
        for hh in heads:
            update(hh, j, sa[hh][...], False)
        for hh in heads:
            sa[hh][...] = scores(hh, j + 2)
        for hh in heads:
            update(hh, j + 1, sb[hh][...], False)
        return carry

    for hh in heads:
        sa[hh][...] = scores(hh, 0)
    lax.fori_loop(0, qi // 2, body, 0)

    @pl.when(qi % 2 == 0)
    def _():
        for hh in heads:
            update(hh, qi, sa[hh][...], True)

    @pl.when(qi % 2 == 1)
    def _():
        for hh in heads:
            sb[hh][...] = scores(hh, qi)
        for hh in heads:
            update(hh, qi - 1, sa[hh][...], False)
        for hh in heads:
            update(hh, qi, sb[hh][...], True)

    lam = _lam(lam_ref, l_init)
    for hh, (_, _, acc_ref, _, _) in enumerate(per_head):
        o = acc_ref[0:DV, :] / acc_ref[DV:DV + 1, :]
        o = o[:, :t] - lam * o[:, t:]
        ms = jnp.mean(o * o, axis=0, keepdims=True)
        y = (o * lax.rsqrt(ms + EPS)).T
        o_ref[:, hh * DV:(hh + 1) * DV] = (y * sub_ref[...] * (1.0 - l_init)).astype(o_ref.dtype)


def _attn_prompt(lam, q, k, v, subln, layer, batch, seq, t, nh):
    nq = seq // t
    kern = functools.partial(_attn_kernel, t=t, nh=nh, l_init=_l_init(layer))
    head_scratch = [pltpu.VMEM((nq, DV + ONES_ROWS, t), BF16),
                    pltpu.VMEM((1, 2 * t), F32),
                    pltpu.VMEM((DV + ONES_ROWS, 2 * t), F32),
                    pltpu.VMEM((t, 2 * t), F32), pltpu.VMEM((t, 2 * t), F32)]
    return pl.pallas_call(
        kern,
        grid=(batch, N_HEADS_A // nh, nq),
        in_specs=[
            pl.BlockSpec((None, 4, DK), lambda b, h, i: (layer, 0, 0)),
            pl.BlockSpec((t, nh * LANES), lambda b, h, i: (b * nq + i, h)),
            pl.BlockSpec((seq, nh * LANES), lambda b, h, i: (b, h)),
            pl.BlockSpec((seq, nh * LANES), lambda b, h, i: (b, h)),
            pl.BlockSpec((None, 1, DV), lambda b, h, i: (layer, 0, 0)),
        ],
        out_specs=pl.BlockSpec((t, nh * LANES), lambda b, h, i: (b * nq + i, h)),
        out_shape=jax.ShapeDtypeStruct((batch * seq, N_HEADS_A * DV), BF16),
        scratch_shapes=head_scratch * nh,
        compiler_params=pltpu.CompilerParams(
            dimension_semantics=("parallel", "parallel", "arbitrary"), vmem_limit_bytes=VMEM_LIMIT),
        name="attn_prompt",
    )(lam, q, k, v, subln)


def _dec_kernel(pt_ref, lam_ref, qm_ref, knew_ref, vnew_ref, sub_ref, *rest, g_pages, t_new, l_init):
    del pt_ref
    k_refs = rest[:g_pages]
    v_refs = rest[g_pages:2 * g_pages]
    o_ref, m_ref, l_ref, acc_ref = rest[2 * g_pages:]
    pg = pl.program_id(1)

    @pl.when(pg == 0)
    def _():
        m_ref[...] = jnp.full_like(m_ref, NEG)
        l_ref[...] = jnp.zeros_like(l_ref)
        acc_ref[...] = jnp.zeros_like(acc_ref)

    qm = qm_ref[...]

    def by_head(refs):
        return jnp.stack([
            jnp.concatenate([r[pl.ds(h, PAGE_SIZE, stride=N_HEADS_A), :] for r in refs], axis=0)
            for h in range(N_HEADS_A)]).astype(BF16)

    def update(s, v3):
        m_old = m_ref[...]
        m_new = jnp.maximum(m_old, jnp.max(s, axis=-1, keepdims=True))
        p = jnp.exp2(s - m_new)
        alpha = jnp.exp2(m_old - m_new)
        l_ref[...] = alpha * l_ref[...] + jnp.sum(p, axis=-1, keepdims=True)
        pv = jnp.einsum('hqk,hkd->hqd', p.astype(BF16), v3, preferred_element_type=F32)
        acc_ref[...] = alpha * acc_ref[...] + pv
        m_ref[...] = m_new

    def scores(k3):
        return jnp.einsum('hqd,hkd->hqk', qm, k3, preferred_element_type=F32)

    update(scores(by_head(k_refs)), by_head(v_refs))

    @pl.when(pg == pl.num_programs(1) - 1)
    def _():
        s_new = scores(knew_ref[...])
        r = lax.broadcasted_iota(jnp.int32, s_new.shape, 1)
        c = lax.broadcasted_iota(jnp.int32, s_new.shape, 2)
        update(jnp.where(c <= (r % t_new), s_new, NEG), vnew_ref[...])

        o = acc_ref[...] / l_ref[...]
        o = o[:, :t_new] - _lam(lam_ref, l_init) * o[:, t_new:]
        y = _rms(o, sub_ref[...]) * (1.0 - l_init)
        for h in range(N_HEADS_A):
            o_ref[:, h * DV:(h + 1) * DV] = y[h]


def _attn_sample(page_table, lam, qm, knew, vnew, subln, cache_k, cache_v, layer, g_pages, t_new):
    db, n_pages = page_table.shape
    width = N_HEADS_A * DV
    page_rows = PAGE_SIZE * N_HEADS_A
    kern = functools.partial(_dec_kernel, g_pages=g_pages, t_new=t_new, l_init=_l_init(layer))

    def page_spec(g):
        return pl.BlockSpec((None, None, page_rows, LANES),
                            lambda b, p, pt: (layer, pt[b, p * g_pages + g], 0, 0))

    kv_specs = [page_spec(g) for g in range(g_pages)]

    per_batch = lambda b, p, pt: (b, 0, 0, 0)
    grid_spec = pltpu.PrefetchScalarGridSpec(
        num_scalar_prefetch=1,
        grid=(db, n_pages // g_pages),
        in_specs=[
            pl.BlockSpec((None, 4, DK), lambda b, p, pt: (layer, 0, 0)),
            pl.BlockSpec((None, N_HEADS_A, 2 * t_new, LANES), per_batch),
            pl.BlockSpec((None, N_HEADS_A, PAGE_SIZE, LANES), per_batch),
            pl.BlockSpec((None, N_HEADS_A, PAGE_SIZE, LANES), per_batch),
            pl.BlockSpec((None, 1, DV), lambda b, p, pt: (layer, 0, 0)),
        ] + kv_specs + kv_specs,
        out_specs=pl.BlockSpec((t_new, width), lambda b, p, pt: (b, 0)),
        scratch_shapes=[pltpu.VMEM((N_HEADS_A, 2 * t_new, 1), F32),
                        pltpu.VMEM((N_HEADS_A, 2 * t_new, 1), F32),
                        pltpu.VMEM((N_HEADS_A, 2 * t_new, DV), F32)],
    )
    return pl.pallas_call(
        kern,
        grid_spec=grid_spec,
        out_shape=jax.ShapeDtypeStruct((db * t_new, width), F32),
        compiler_params=pltpu.CompilerParams(
            dimension_semantics=("parallel", "arbitrary"), vmem_limit_bytes=VMEM_LIMIT),
        name="attn_sample",
    )(page_table, lam, qm, knew, vnew, subln, *([cache_k] * len(kv_specs)), *([cache_v] * len(kv_specs)))


def _post_kernel(h_ref, a_ref, up_ref, halo_ref, ug_ref, vg_ref, pw_ref, ps_ref, ws_ref, gb_ref,
                 wo_ref, o_ref, ext_ref, cat_ref, *, tm, tiles_per_seq, d_attn, d_pool):
    i = pl.program_id(0)

    halo = halo_ref[...]
    if tiles_per_seq is not None:
        seq_tile = i % tiles_per_seq
        halo = jnp.where(seq_tile == 0, jnp.zeros_like(halo), halo)
        pos = seq_tile * tm + lax.broadcasted_iota(jnp.int32, (tm, LANES), 0)
    ext_ref[0:HALO, :] = halo
    ext_ref[HALO:HALO + tm, :] = up_ref[...]
    for g, w in enumerate(POOL_WINDOWS):
        sl = slice(g * LANES, (g + 1) * LANES)
        u = up_ref[:, sl]
        tot = u
        for k in range(1, w):
            tot = tot + ext_ref[HALO - k:HALO - k + tm, sl]
        if tiles_per_seq is not None:
            cnt = jnp.minimum(w, pos + 1).astype(F32)
            mean = tot / cnt
        else:
            mean = tot / float(w)
        y = _dot((mean - u).astype(BF16), pw_ref[g]) * ps_ref[:, sl]
        cat_ref[:, d_attn + g * LANES:d_attn + (g + 1) * LANES] = y.astype(BF16)

    r = lax.broadcasted_iota(jnp.int32, (CHUNK, CHUNK), 0)
    c = lax.broadcasted_iota(jnp.int32, (CHUNK, CHUNK), 1)
    o_g = d_attn + d_pool
    for hh in range(N_HEADS_C):
        wsc = jnp.where(r >= c, ws_ref[hh], 0.0).astype(BF16)
        bias = gb_ref[hh]
        cs = slice(hh * CHUNK, (hh + 1) * CHUNK)
        for n in range(tm // CHUNK):
            rs = slice(n * CHUNK, (n + 1) * CHUNK)
            gate = _dot(wsc, vg_ref[rs, cs].astype(BF16)) + bias
            cat_ref[rs, o_g + hh * CHUNK:o_g + (hh + 1) * CHUNK] = (ug_ref[rs, cs] * gate).astype(BF16)

    cat_ref[:, 0:d_attn] = a_ref[...]
    o_ref[...] = h_ref[...] + _dot(cat_ref[...], wo_ref[...])


def _post(h, a, up, halo_src, ug, vg, pool_w, pool_scale, ws, gb, w_out, layer, tm, tiles_per_seq):
    m, d = h.shape
    d_attn = a.shape[1]
    d_pool = up.shape[1]
    d_gmlp = ug.shape[1]
    row = lambda i: (i, 0)
    if tiles_per_seq is not None:
        halo_map = lambda i: (jnp.maximum(i * (tm // HALO) - 1, 0), 0)
    else:
        halo_map = row
    kern = functools.partial(_post_kernel, tm=tm, tiles_per_seq=tiles_per_seq, d_attn=d_attn,
                             d_pool=d_pool)
    return pl.pallas_call(
        kern,
        grid=(m // tm,),
        in_specs=[
            pl.BlockSpec((tm, d), row),
            pl.BlockSpec((tm, d_attn), row),
            pl.BlockSpec((tm, d_pool), row),
            pl.BlockSpec((HALO, d_pool), halo_map),
            pl.BlockSpec((tm, d_gmlp), row),
            pl.BlockSpec((tm, d_gmlp), row),
            pl.BlockSpec((None, len(POOL_WINDOWS), LANES, LANES), lambda i: (layer, 0, 0, 0)),
            pl.BlockSpec((None, 1, d_pool), lambda i: (layer, 0, 0)),
            pl.BlockSpec((None, N_HEADS_C, CHUNK, CHUNK), lambda i: (layer, 0, 0, 0)),
            pl.BlockSpec((None, N_HEADS_C, CHUNK, CHUNK), lambda i: (layer, 0, 0, 0)),
            pl.BlockSpec((None, d, d), lambda i: (layer, 0, 0)),
        ],
        out_specs=pl.BlockSpec((tm, d), row),
        out_shape=jax.ShapeDtypeStruct((m, d), F32),
        scratch_shapes=[pltpu.VMEM((HALO + tm, d_pool), F32), pltpu.VMEM((tm, d), BF16)],
        compiler_params=pltpu.CompilerParams(
            dimension_semantics=("parallel",), vmem_limit_bytes=VMEM_LIMIT),
        name="post",
    )(h, a, up, halo_src, ug, vg, pool_w, pool_scale, ws, gb, w_out)


def _rope_tables(pos):
    half = DK // 2
    inv_freq = jnp.power(ROPE_THETA, -jnp.arange(half, dtype=F32) / half)
    ang = pos.astype(F32)[:, None] * inv_freq[None, :]
    cos = jnp.tile(jnp.cos(ang), (1, LANES // half))
    sign = jnp.where((jnp.arange(LANES) % DK) < half, -1.0, 1.0).astype(F32)
    sin = jnp.tile(jnp.sin(ang), (1, LANES // half)) * sign[None, :]
    return cos, sin


def kernel(x_prompt, x_sample, cache_k, cache_v, state_pool, page_table, norm_ffn1, w1_gate, w1_up, w1_down, norm_mix, w_in, w_out, q_norm, k_norm, lambda_q1, lambda_k1, lambda_q2, lambda_k2, subln, pool_w, pool_scale, gmlp_norm, gmlp_ws, gmlp_b, norm_ffn2, w2_gate, w2_up, w2_down):
    b, seq, d = x_prompt.shape
    db, t_new, _ = x_sample.shape
    depth = w_in.shape[0]
    n_pages = page_table.shape[1]
    past = n_pages * PAGE_SIZE
    width = N_HEADS_A * DV
    d_pool = pool_scale.shape[1]
    d_gmlp = gmlp_norm.shape[1]
    ms = db * t_new

    tm_ffn, tf = 1024, 512
    tm_proj = 256
    t_attn = 512
    heads_per_step = 4
    tm_post = 256
    g_pages = 16

    bf = lambda w: w.astype(BF16)
    w1g, w1u, w1d = w1_gate, w1_up, w1_down
    w2g, w2u, w2d = w2_gate, w2_up, w2_down
    w_in_b, w_out_b, pool_w_b = bf(w_in), bf(w_out), bf(pool_w)
    row3 = lambda p: p.reshape(depth, 1, -1)
    n1, nm, n2 = row3(norm_ffn1), row3(norm_mix), row3(norm_ffn2)
    qg = row3(jnp.tile(q_norm, (1, LANES // DK)))
    kg = row3(jnp.tile(k_norm, (1, LANES // DK)))
    gg, sub, ps = row3(gmlp_norm), row3(subln), row3(pool_scale)
    lam = jnp.stack([lambda_q1, lambda_k1, lambda_q2, lambda_k2], axis=1)
    gb = jnp.broadcast_to(gmlp_b[:, :, :, None], gmlp_b.shape + (CHUNK,))
    grp = jnp.arange(LANES) // DK
    avg = jnp.where(grp[:, None] == grp[None, :], 1.0 / DK, 0.0).astype(BF16)

    cos_p, sin_p = _rope_tables(jnp.arange(seq, dtype=jnp.int32))
    cos_s, sin_s = _rope_tables(past + (jnp.arange(ms, dtype=jnp.int32) % t_new))


    ck = cache_k.reshape(depth, cache_k.shape[1], PAGE_SIZE * N_HEADS_A, LANES)
    cv = cache_v.reshape(depth, cache_v.shape[1], PAGE_SIZE * N_HEADS_A, LANES)
    new_pad = ((0, 0), (0, 0), (0, PAGE_SIZE - t_new), (0, 0))

    hp = x_prompt.reshape(b * seq, d)
    hs = x_sample.reshape(ms, d)
    kp_l, vp_l, poolp_l, ks_l, vs_l, pools_l, gv_l = [], [], [], [], [], [], []
    pad_rows = lambda x: jnp.pad(x.reshape(db, t_new, -1), ((0, 0), (0, CHUNK - t_new), (0, 0))
                                 ).reshape(db * CHUNK, -1)

    for l in range(depth):
        hp, hs = _ffn(hp, hs, n1, w1g, w1u, w1d, l, tm_ffn, tf)

        proj_out = _proj(hp, hs, nm, w_in_b, qg, kg, gg, cos_p, sin_p, cos_s, sin_s, avg, l,
                         tm_proj, seq // tm_proj)

        q, k, kb, v, vb, up, ug, vg = proj_out[:8]
        a = _attn_prompt(lam, q, kb, vb, sub, l, b, seq, t_attn, heads_per_step)
        hp = _post(hp, a, up, up, ug, vg, pool_w_b, ps, gmlp_ws, gb, w_out_b, l, tm_post,
                   seq // tm_post)
        kp_l.append(k)
        vp_l.append(v)
        poolp_l.append(up.reshape(b, seq, d_pool)[:, seq - POOL_BUF:])

        q, k, kb, v, vb, us, ug, vg = proj_out[8:]
        q5 = q.reshape(db, t_new, N_HEADS_A, 2, DK).transpose(0, 2, 1, 3, 4)
        comp = jnp.arange(2)
        qm = jnp.where(comp[None, None, :, None, None, None] == comp[None, None, None, None, :, None],
                       q5[:, :, None], jnp.zeros((), BF16))
        qm = qm.reshape(db, N_HEADS_A, 2 * t_new, LANES)
        knew = jnp.pad(kb.reshape(db, t_new, N_HEADS_A, LANES).transpose(0, 2, 1, 3), new_pad)
        vnew = jnp.pad(vb.reshape(db, t_new, N_HEADS_A, LANES).transpose(0, 2, 1, 3), new_pad)
        a = _attn_sample(page_table, lam, qm, knew, vnew, sub, ck, cv, l, g_pages, t_new)
        ext = jnp.concatenate([state_pool[l], us.reshape(db, t_new, d_pool)], axis=1)
        halo = jnp.pad(state_pool[l], ((0, 0), (HALO - POOL_BUF, 0), (0, 0))).reshape(db * HALO, d_pool)
        hs_pad = _post(pad_rows(hs), pad_rows(a.astype(BF16)), pad_rows(us), halo, pad_rows(ug),
                       pad_rows(vg), pool_w_b, ps, gmlp_ws, gb, w_out_b, l, CHUNK, None)
        hs = hs_pad.reshape(db, CHUNK, d)[:, :t_new].reshape(ms, d)
        ks_l.append(k)
        vs_l.append(v)
        pools_l.append(ext[:, t_new:])
        gv_l.append(vg)

        hp, hs = _ffn(hp, hs, n2, w2g, w2u, w2d, l, tm_ffn, tf)

    n_pg = seq // PAGE_SIZE
    return (
        hp.reshape(b, seq, d),
        hs.reshape(db, t_new, d),
        jnp.stack(kp_l).reshape(depth, b, n_pg, PAGE_SIZE, N_HEADS_A, 2 * DK),
        jnp.stack(vp_l).reshape(depth, b, n_pg, PAGE_SIZE, N_HEADS_A, DV),
        jnp.stack(poolp_l),
        jnp.stack(ks_l).reshape(depth, db, t_new, N_HEADS_A, 2 * DK),
        jnp.stack(vs_l).reshape(depth, db, t_new, N_HEADS_A, DV),
        jnp.stack(pools_l),
        jnp.stack(gv_l).reshape(depth, db, t_new, d_gmlp),
    )
```

```python
import functools
import math

import jax
import jax.numpy as jnp
from jax import lax
from jax.experimental import pallas as pl
from jax.experimental.pallas import tpu as pltpu

F32 = jnp.float32
BF16 = jnp.bfloat16

EPS = 1e-6
ROPE_THETA = 10000.0
N_HEADS_A = 8
DV = 128
DK = 64
POOL_WINDOWS = (2, 4, 8, 16)
POOL_BUF = 15
HALO = 16
CHUNK = 128
N_HEADS_C = 4
PAGE_SIZE = 128
LANES = 128
NEG = -1e30
ONES_ROWS = 16
VMEM_LIMIT = 56 * 1024 * 1024

_NT = (((1,), (1,)), ((), ()))


def _dot(a, b):
    return jnp.dot(a, b, preferred_element_type=F32)


def _dot_nt(a, b):
    return lax.dot_general(a, b, _NT, preferred_element_type=F32)


def _rms(x, g):
    ms = jnp.mean(x * x, axis=-1, keepdims=True)
    return x * lax.rsqrt(ms + EPS) * g


def _gelu(x):
    return 0.5 * x * (1.0 + lax.erf(x * (2.0 ** -0.5)))


def _l_init(layer):
    return 0.8 - 0.6 * math.exp(-0.3 * layer)


def _lam(lam_ref, l_init):
    lp = lam_ref[...]
    s1 = jnp.sum(lp[0:1] * lp[1:2], axis=-1, keepdims=True)
    s2 = jnp.sum(lp[2:3] * lp[3:4], axis=-1, keepdims=True)
    return jnp.exp(s1) - jnp.exp(s2) + l_init


def _ffn_kernel(x_ref, xs_ref, g_ref, wg_ref, wu_ref, wd_ref, o_ref, os_ref, xn_ref, *, tm):
    i = pl.program_id(0)
    j = pl.program_id(1)
    last_j = pl.num_programs(1) - 1

    def hidden(xn):
        gate = _dot(xn, wg_ref[...].astype(BF16))
        up = _dot(xn, wu_ref[...].astype(BF16))
        return (gate * jax.nn.sigmoid(gate) * up).astype(BF16)

    @pl.when(j == 0)
    def _():
        xn_ref[0:tm, :] = _rms(x_ref[...], g_ref[...]).astype(BF16)
        o_ref[...] = jnp.zeros_like(o_ref)

    @pl.when((j == 0) & (i == 0))
    def _():
        xn_ref[tm:, :] = _rms(xs_ref[...], g_ref[...]).astype(BF16)
        os_ref[...] = jnp.zeros_like(os_ref)

    @pl.when(i == 0)
    def _():
        act = hidden(xn_ref[...])
        wd = wd_ref[...].astype(BF16)
        o_ref[...] += _dot(act[:tm], wd)
        os_ref[...] += _dot(act[tm:], wd)

    @pl.when(i > 0)
    def _():
        o_ref[...] += _dot(hidden(xn_ref[0:tm, :]), wd_ref[...].astype(BF16))

    @pl.when(j == last_j)
    def _():
        o_ref[...] = x_ref[...] + 0.5 * o_ref[...]

    @pl.when((j == last_j) & (i == 0))
    def _():
        os_ref[...] = xs_ref[...] + 0.5 * os_ref[...]


def _ffn(x, xs, g, wg, wu, wd, layer, tm, tf):
    m, d = x.shape
    ms = xs.shape[0]
    f = wg.shape[-1]
    return pl.pallas_call(
        functools.partial(_ffn_kernel, tm=tm),
        grid=(m // tm, f // tf),
        in_specs=[
            pl.BlockSpec((tm, d), lambda i, j: (i, 0)),
            pl.BlockSpec((ms, d), lambda i, j: (0, 0)),
            pl.BlockSpec((None, 1, d), lambda i, j: (layer, 0, 0)),
            pl.BlockSpec((None, d, tf), lambda i, j: (layer, 0, j)),
            pl.BlockSpec((None, d, tf), lambda i, j: (layer, 0, j)),
            pl.BlockSpec((None, tf, d), lambda i, j: (layer, j, 0)),
        ],
        out_specs=[pl.BlockSpec((tm, d), lambda i, j: (i, 0)),
                   pl.BlockSpec((ms, d), lambda i, j: (0, 0))],
        out_shape=[jax.ShapeDtypeStruct((m, d), F32), jax.ShapeDtypeStruct((ms, d), F32)],
        scratch_shapes=[pltpu.VMEM((tm + ms, d), BF16)],
        compiler_params=pltpu.CompilerParams(
            dimension_semantics=("arbitrary", "arbitrary"), vmem_limit_bytes=VMEM_LIMIT),
        name="ffn",
    )(x, xs, g, wg, wu, wd)


def _proj_kernel(x_ref, xs_ref, g_ref, w_ref, qg_ref, kg_ref, gg_ref, cos_ref, sin_ref, coss_ref, sins_ref,
                 avg_ref, *out_refs, tm, d_attn, d_pool, d_gmlp, attn_scale):
    n_out = len(out_refs) // 2
    avg = avg_ref[...]

    def body(x, cos, sin, store):
        rows = x.shape[0]
        xn = _rms(x, g_ref[...]).astype(BF16)
        lane = lax.broadcasted_iota(jnp.int32, (rows, LANES), 1)
        first_half = (lane % DK) < (DK // 2)

        def norm_rope(z, gain):
            z2 = z * z
            hi = z2.astype(BF16)
            lo = (z2 - hi.astype(F32)).astype(BF16)
            ms = _dot(hi, avg) + _dot(lo, avg)
            y = z * lax.rsqrt(ms + EPS) * gain
            partner = jnp.where(first_half, pltpu.roll(y, LANES - DK // 2, 1), pltpu.roll(y, DK // 2, 1))
            return y * cos + partner * sin

        zq = _dot(xn, w_ref[:, 0:d_attn])
        qg = qg_ref[...]
        for c in range(d_attn // LANES):
            sl = slice(c * LANES, (c + 1) * LANES)
            store(0, sl, (norm_rope(zq[:, sl], qg) * attn_scale).astype(BF16))

        zk = _dot(xn, w_ref[:, d_attn:2 * d_attn])
        kg = kg_ref[...]
        for c in range(d_attn // LANES):
            sl = slice(c * LANES, (c + 1) * LANES)
            kr = norm_rope(zk[:, sl], kg)
            store(1, sl, kr)
            store(2, sl, kr.astype(BF16))

        zv = _dot(xn, w_ref[:, 2 * d_attn:3 * d_attn])
        store(3, slice(None), zv)
        store(4, slice(None), zv.astype(BF16))

        o = 3 * d_attn
        store(5, slice(None), _dot(xn, w_ref[:, o:o + d_pool]))
        o += d_pool
        store(6, slice(None), _gelu(_dot(xn, w_ref[:, o:o + d_gmlp])))
        o += d_gmlp
        zg = _gelu(_dot(xn, w_ref[:, o:o + d_gmlp]))
        store(7, slice(None), _rms(zg, gg_ref[...]))

    def store_tile(idx, cols, val):
        out_refs[idx][:, cols] = val

    def store_both(idx, cols, val):
        out_refs[idx][:, cols] = val[:tm]
        out_refs[n_out + idx][:, cols] = val[tm:]

    i = pl.program_id(0)

    @pl.when(i == 0)
    def _():
        body(jnp.concatenate([x_ref[...], xs_ref[...]], axis=0),
             jnp.concatenate([cos_ref[...], coss_ref[...]], axis=0),
             jnp.concatenate([sin_ref[...], sins_ref[...]], axis=0), store_both)

    @pl.when(i > 0)
    def _():
        body(x_ref[...], cos_ref[...], sin_ref[...], store_tile)


def _proj(x, xs, g, w_in, qg, kg, gg, cos, sin, cos_s, sin_s, avg, layer, tm, table_tiles):
    m, d = x.shape
    ms = xs.shape[0]
    d_in = w_in.shape[-1]
    d_attn = N_HEADS_A * DV
    d_pool = 4 * LANES
    d_gmlp = N_HEADS_C * CHUNK
    row = lambda i: (i, 0)
    fixed = lambda i: (0, 0)
    tab = lambda i: (i % table_tiles, 0)
    par = lambda i: (layer, 0, 0)
    kern = functools.partial(_proj_kernel, tm=tm, d_attn=d_attn, d_pool=d_pool, d_gmlp=d_gmlp,
                             attn_scale=DK ** -0.5 * math.log2(math.e))
    widths = (d_attn, d_attn, d_attn, d_attn, d_attn, d_pool, d_gmlp, d_gmlp)
    dtypes = (BF16, F32, BF16, F32, BF16, F32, F32, F32)
    return pl.pallas_call(
        kern,
        grid=(m // tm,),
        in_specs=[
            pl.BlockSpec((tm, d), row),
            pl.BlockSpec((ms, d), fixed),
            pl.BlockSpec((None, 1, d), par),
            pl.BlockSpec((None, d, d_in), par),
            pl.BlockSpec((None, 1, LANES), par),
            pl.BlockSpec((None, 1, LANES), par),
            pl.BlockSpec((None, 1, d_gmlp), par),
            pl.BlockSpec((tm, LANES), tab),
            pl.BlockSpec((tm, LANES), tab),
            pl.BlockSpec((ms, LANES), fixed),
            pl.BlockSpec((ms, LANES), fixed),
            pl.BlockSpec((LANES, LANES), fixed),
        ],
        out_specs=[pl.BlockSpec((tm, w), row) for w in widths] + [pl.BlockSpec((ms, w), fixed) for w in widths],
        out_shape=[jax.ShapeDtypeStruct((m, w), dt) for w, dt in zip(widths, dtypes)]
        + [jax.ShapeDtypeStruct((ms, w), dt) for w, dt in zip(widths, dtypes)],
        compiler_params=pltpu.CompilerParams(
            dimension_semantics=("arbitrary",), vmem_limit_bytes=VMEM_LIMIT),
        name="proj",
    )(x, xs, g, w_in, qg, kg, gg, cos, sin, cos_s, sin_s, avg)


def _attn_kernel(lam_ref, q_ref, k_ref, v_ref, sub_ref, o_ref, *scratch, t, nh, l_init):
    qi = pl.program_id(2)
    per_head = [scratch[5 * hh:5 * hh + 5] for hh in range(nh)]

    @pl.when(qi == 0)
    def _():
        for hh, (vt_ref, *_) in enumerate(per_head):
            for jb in range(vt_ref.shape[0]):
                vt_ref[jb, 0:DV, :] = v_ref[jb * t:(jb + 1) * t, hh * DV:(hh + 1) * DV].astype(F32).T.astype(BF16)
                vt_ref[jb, DV:, :] = jnp.ones((ONES_ROWS, t), BF16)

    lane = lax.broadcasted_iota(jnp.int32, (t, LANES), 1)
    q2s = []
    for hh, (_, m_ref, acc_ref, _, _) in enumerate(per_head):
        q = q_ref[:, hh * LANES:(hh + 1) * LANES]
        zero = jnp.zeros_like(q)
        q2s.append(jnp.concatenate([jnp.where(lane < DK, q, zero), jnp.where(lane >= DK, q, zero)], axis=0))
        m_ref[...] = jnp.full_like(m_ref, NEG)
        acc_ref[...] = jnp.zeros_like(acc_ref)

    def scores(hh, j):
        kj = k_ref[pl.ds(pl.multiple_of(j * t, t), t), hh * LANES:(hh + 1) * LANES]
        return _dot_nt(kj, q2s[hh])

    def update(hh, j, s, masked):
        vt_ref, m_ref, acc_ref, _, _ = per_head[hh]
        if masked:
            r = lax.broadcasted_iota(jnp.int32, s.shape, 0)
            c = lax.broadcasted_iota(jnp.int32, s.shape, 1)
            s = jnp.where(r <= jnp.where(c >= t, c - t, c), s, NEG)
        m_old = m_ref[...]
        m_new = jnp.maximum(m_old, jnp.max(s, axis=0, keepdims=True))
        p = jnp.exp2(s - m_new)
        alpha = jnp.exp2(m_old - m_new)
        acc_ref[...] = alpha * acc_ref[...] + _dot(vt_ref[j], p.astype(BF16))
        m_ref[...] = m_new

    heads = range(nh)
    sa = [ph[3] for ph in per_head]
    sb = [ph[4] for ph in per_head]

    def body(i2, carry):
        j = 2 * i2
        for hh in heads:
            sb[hh][...] = scores(hh, j + 1)
        for hh in heads:
            update(hh, j, sa[hh][...], False)
        for hh in heads:
            sa[hh][...] = scores(hh, j + 2)
        for hh in heads:
            update(hh, j + 1, sb[hh][...], False)
        return carry

    for hh in heads:
        sa[hh][...] = scores(hh, 0)
    lax.fori_loop(0, qi // 2, body, 0)

    @pl.when(qi % 2 == 0)
    def _():
        for hh in heads:
            update(hh, qi, sa[hh][...], True)

    @pl.when(qi % 2 == 1)
    def _():
        for hh in heads:
            sb[hh][...] = scores(hh, qi)
        for hh in heads:
            update(hh, qi - 1, sa[hh][...], False)
        for hh in heads:
            update(hh, qi, sb[hh][...], True)

    lam = _lam(lam_ref, l_init)
    for hh, (_, _, acc_ref, _, _) in enumerate(per_head):
        o = acc_ref[0:DV, :] / acc_ref[DV:DV + 1, :]
        o = o[:, :t] - lam * o[:, t:]
        ms = jnp.mean(o * o, axis=0, keepdims=True)
        y = (o * lax.rsqrt(ms + EPS)).T
        o_ref[:, hh * DV:(hh + 1) * DV] = (y * sub_ref[...] * (1.0 - l_init)).astype(o_ref.dtype)


def _attn_prompt(lam, q, k, v, subln, layer, batch, seq, t, nh):
    nq = seq // t
    kern = functools.partial(_attn_kernel, t=t, nh=nh, l_init=_l_init(layer))
    head_scratch = [pltpu.VMEM((nq, DV + ONES_ROWS, t), BF16),
                    pltpu.VMEM((1, 2 * t), F32),
                    pltpu.VMEM((DV + ONES_ROWS, 2 * t), F32),
                    pltpu.VMEM((t, 2 * t), F32), pltpu.VMEM((t, 2 * t), F32)]
    return pl.pallas_call(
        kern,
        grid=(batch, N_HEADS_A // nh, nq),
        in_specs=[
            pl.BlockSpec((None, 4, DK), lambda b, h, i: (layer, 0, 0)),
            pl.BlockSpec((t, nh * LANES), lambda b, h, i: (b * nq + i, h)),
            pl.BlockSpec((seq, nh * LANES), lambda b, h, i: (b, h)),
            pl.BlockSpec((seq, nh * LANES), lambda b, h, i: (b, h)),
            pl.BlockSpec((None, 1, DV), lambda b, h, i: (layer, 0, 0)),
        ],
        out_specs=pl.BlockSpec((t, nh * LANES), lambda b, h, i: (b * nq + i, h)),
        out_shape=jax.ShapeDtypeStruct((batch * seq, N_HEADS_A * DV), BF16),
        scratch_shapes=head_scratch * nh,
        compiler_params=pltpu.CompilerParams(
            dimension_semantics=("parallel", "parallel", "arbitrary"), vmem_limit_bytes=VMEM_LIMIT),
        name="attn_prompt",
    )(lam, q, k, v, subln)


def _dec_kernel(pt_ref, lam_ref, qm_ref, knew_ref, vnew_ref, sub_ref, *rest, g_pages, t_new, l_init):
    del pt_ref
    k_refs = rest[:g_pages]
    v_refs = rest[g_pages:2 * g_pages]
    o_ref, m_ref, l_ref, acc_ref = rest[2 * g_pages:]
    pg = pl.program_id(1)

    @pl.when(pg == 0)
    def _():
        m_ref[...] = jnp.full_like(m_ref, NEG)
        l_ref[...] = jnp.zeros_like(l_ref)
        acc_ref[...] = jnp.zeros_like(acc_ref)

    qm = qm_ref[...]

    def by_head(refs):
        return jnp.concatenate([
            pltpu.einshape("khd->hkd", r[...].reshape(PAGE_SIZE, N_HEADS_A, LANES)) for r in refs],
            axis=1).astype(BF16)

    def update(s, v3):
        m_old = m_ref[...]
        m_new = jnp.maximum(m_old, jnp.max(s, axis=-1, keepdims=True))
        p = jnp.exp2(s - m_new)
        alpha = jnp.exp2(m_old - m_new)
        l_ref[...] = alpha * l_ref[...] + jnp.sum(p, axis=-1, keepdims=True)
        pv = jnp.einsum('hqk,hkd->hqd', p.astype(BF16), v3, preferred_element_type=F32)
        acc_ref[...] = alpha * acc_ref[...] + pv
        m_ref[...] = m_new

    def scores(k3):
        return jnp.einsum('hqd,hkd->hqk', qm, k3, preferred_element_type=F32)

    update(scores(by_head(k_refs)), by_head(v_refs))

    @pl.when(pg == pl.num_programs(1) - 1)
    def _():
        s_new = scores(knew_ref[...])
        r = lax.broadcasted_iota(jnp.int32, s_new.shape, 1)
        c = lax.broadcasted_iota(jnp.int32, s_new.shape, 2)
        update(jnp.where(c <= (r % t_new), s_new, NEG), vnew_ref[...])

        o = acc_ref[...] / l_ref[...]
        o = o[:, :t_new] - _lam(lam_ref, l_init) * o[:, t_new:]
        y = _rms(o, sub_ref[...]) * (1.0 - l_init)
        for h in range(N_HEADS_A):
            o_ref[:, h * DV:(h + 1) * DV] = y[h]


def _attn_sample(page_table, lam, qm, knew, vnew, subln, cache_k, cache_v, layer, g_pages, t_new):
    db, n_pages = page_table.shape
    width = N_HEADS_A * DV
    page_rows = PAGE_SIZE * N_HEADS_A
    kern = functools.partial(_dec_kernel, g_pages=g_pages, t_new=t_new, l_init=_l_init(layer))

    def page_spec(g):
        return pl.BlockSpec((None, None, page_rows, LANES),
                            lambda b, p, pt: (layer, pt[b, p * g_pages + g], 0, 0))

    kv_specs = [page_spec(g) for g in range(g_pages)]

    per_batch = lambda b, p, pt: (b, 0, 0, 0)
    grid_spec = pltpu.PrefetchScalarGridSpec(
        num_scalar_prefetch=1,
        grid=(db, n_pages // g_pages),
        in_specs=[
            pl.BlockSpec((None, 4, DK), lambda b, p, pt: (layer, 0, 0)),
            pl.BlockSpec((None, N_HEADS_A, 2 * t_new, LANES), per_batch),
            pl.BlockSpec((None, N_HEADS_A, PAGE_SIZE, LANES), per_batch),
            pl.BlockSpec((None, N_HEADS_A, PAGE_SIZE, LANES), per_batch),
            pl.BlockSpec((None, 1, DV), lambda b, p, pt: (layer, 0, 0)),
        ] + kv_specs + kv_specs,
        out_specs=pl.BlockSpec((t_new, width), lambda b, p, pt: (b, 0)),
        scratch_shapes=[pltpu.VMEM((N_HEADS_A, 2 * t_new, 1), F32),
                        pltpu.VMEM((N_HEADS_A, 2 * t_new, 1), F32),
                        pltpu.VMEM((N_HEADS_A, 2 * t_new, DV), F32)],
    )
    return pl.pallas_call(
        kern,
        grid_spec=grid_spec,
        out_shape=jax.ShapeDtypeStruct((db * t_new, width), F32),
        compiler_params=pltpu.CompilerParams(
            dimension_semantics=("parallel", "arbitrary"), vmem_limit_bytes=VMEM_LIMIT),
        name="attn_sample",
    )(page_table, lam, qm, knew, vnew, subln, *([cache_k] * len(kv_specs)), *([cache_v] * len(kv_specs)))


def _post_kernel(h_ref, a_ref, up_ref, halo_ref, ug_ref, vg_ref, pw_ref, ps_ref, ws_ref, gb_ref,
                 wo_ref, o_ref, ext_ref, cat_ref, *, tm, tiles_per_seq, d_attn, d_pool):
    i = pl.program_id(0)

    halo = halo_ref[...]
    if tiles_per_seq is not None:
        seq_tile = i % tiles_per_seq
        halo = jnp.where(seq_tile == 0, jnp.zeros_like(halo), halo)
        pos = seq_tile * tm + lax.broadcasted_iota(jnp.int32, (tm, LANES), 0)
    ext_ref[0:HALO, :] = halo
    ext_ref[HALO:HALO + tm, :] = up_ref[...]
    for g, w in enumerate(POOL_WINDOWS):
        sl = slice(g * LANES, (g + 1) * LANES)
        u = up_ref[:, sl]
        tot = u
        for k in range(1, w):
            tot = tot + ext_ref[HALO - k:HALO - k + tm, sl]
        if tiles_per_seq is not None:
            cnt = jnp.minimum(w, pos + 1).astype(F32)
            mean = tot / cnt
        else:
            mean = tot / float(w)
        y = _dot((mean - u).astype(BF16), pw_ref[g]) * ps_ref[:, sl]
        cat_ref[:, d_attn + g * LANES:d_attn + (g + 1) * LANES] = y.astype(BF16)

    r = lax.broadcasted_iota(jnp.int32, (CHUNK, CHUNK), 0)
    c = lax.broadcasted_iota(jnp.int32, (CHUNK, CHUNK), 1)
    o_g = d_attn + d_pool
    for hh in range(N_HEADS_C):
        wsc = jnp.where(r >= c, ws_ref[hh], 0.0).astype(BF16)
        bias = gb_ref[hh]
        cs = slice(hh * CHUNK, (hh + 1) * CHUNK)
        for n in range(tm // CHUNK):
            rs = slice(n * CHUNK, (n + 1) * CHUNK)
            gate = _dot(wsc, vg_ref[rs, cs].astype(BF16)) + bias
            cat_ref[rs, o_g + hh * CHUNK:o_g + (hh + 1) * CHUNK] = (ug_ref[rs, cs] * gate).astype(BF16)

    cat_ref[:, 0:d_attn] = a_ref[...]
    o_ref[...] = h_ref[...] + _dot(cat_ref[...], wo_ref[...])


def _post(h, a, up, halo_src, ug, vg, pool_w, pool_scale, ws, gb, w_out, layer, tm, tiles_per_seq):
    m, d = h.shape
    d_attn = a.shape[1]
    d_pool = up.shape[1]
    d_gmlp = ug.shape[1]
    row = lambda i: (i, 0)
    if tiles_per_seq is not None:
        halo_map = lambda i: (jnp.maximum(i * (tm // HALO) - 1, 0), 0)
    else:
        halo_map = row
    kern = functools.partial(_post_kernel, tm=tm, tiles_per_seq=tiles_per_seq, d_attn=d_attn,
                             d_pool=d_pool)
    return pl.pallas_call(
        kern,
        grid=(m // tm,),
        in_specs=[
            pl.BlockSpec((tm, d), row),
            pl.BlockSpec((tm, d_attn), row),
            pl.BlockSpec((tm, d_pool), row),
            pl.BlockSpec((HALO, d_pool), halo_map),
            pl.BlockSpec((tm, d_gmlp), row),
            pl.BlockSpec((tm, d_gmlp), row),
            pl.BlockSpec((None, len(POOL_WINDOWS), LANES, LANES), lambda i: (layer, 0, 0, 0)),
            pl.BlockSpec((None, 1, d_pool), lambda i: (layer, 0, 0)),
            pl.BlockSpec((None, N_HEADS_C, CHUNK, CHUNK), lambda i: (layer, 0, 0, 0)),
            pl.BlockSpec((None, N_HEADS_C, CHUNK, CHUNK), lambda i: (layer, 0, 0, 0)),
            pl.BlockSpec((None, d, d), lambda i: (layer, 0, 0)),
        ],
        out_specs=pl.BlockSpec((tm, d), row),
        out_shape=jax.ShapeDtypeStruct((m, d), F32),
        scratch_shapes=[pltpu.VMEM((HALO + tm, d_pool), F32), pltpu.VMEM((tm, d), BF16)],
        compiler_params=pltpu.CompilerParams(
            dimension_semantics=("parallel",), vmem_limit_bytes=VMEM_LIMIT),
        name="post",
    )(h, a, up, halo_src, ug, vg, pool_w, pool_scale, ws, gb, w_out)


def _rope_tables(pos):
    half = DK // 2
    inv_freq = jnp.power(ROPE_THETA, -jnp.arange(half, dtype=F32) / half)
    ang = pos.astype(F32)[:, None] * inv_freq[None, :]
    cos = jnp.tile(jnp.cos(ang), (1, LANES // half))
    sign = jnp.where((jnp.arange(LANES) % DK) < half, -1.0, 1.0).astype(F32)
    sin = jnp.tile(jnp.sin(ang), (1, LANES // half)) * sign[None, :]
    return cos, sin


def kernel(x_prompt, x_sample, cache_k, cache_v, state_pool, page_table, norm_ffn1, w1_gate, w1_up, w1_down, norm_mix, w_in, w_out, q_norm, k_norm, lambda_q1, lambda_k1, lambda_q2, lambda_k2, subln, pool_w, pool_scale, gmlp_norm, gmlp_ws, gmlp_b, norm_ffn2, w2_gate, w2_up, w2_down):
    b, seq, d = x_prompt.shape
    db, t_new, _ = x_sample.shape
    depth = w_in.shape[0]
    n_pages = page_table.shape[1]
    past = n_pages * PAGE_SIZE
    width = N_HEADS_A * DV
    d_pool = pool_scale.shape[1]
    d_gmlp = gmlp_norm.shape[1]
    ms = db * t_new

    tm_ffn, tf = 1024, 256
    tm_proj = 256
    t_attn = 512
    heads_per_step = 4
    tm_post = 256
    g_pages = 16

    bf = lambda w: w.astype(BF16)
    w1g, w1u, w1d = w1_gate, w1_up, w1_down
    w2g, w2u, w2d = w2_gate, w2_up, w2_down
    w_in_b, w_out_b, pool_w_b = bf(w_in), bf(w_out), bf(pool_w)
    row3 = lambda p: p.reshape(depth, 1, -1)
    n1, nm, n2 = row3(norm_ffn1), row3(norm_mix), row3(norm_ffn2)
    qg = row3(jnp.tile(q_norm, (1, LANES // DK)))
    kg = row3(jnp.tile(k_norm, (1, LANES // DK)))
    gg, sub, ps = row3(gmlp_norm), row3(subln), row3(pool_scale)
    lam = jnp.stack([lambda_q1, lambda_k1, lambda_q2, lambda_k2], axis=1)
    gb = jnp.broadcast_to(gmlp_b[:, :, :, None], gmlp_b.shape + (CHUNK,))
    grp = jnp.arange(LANES) // DK
    avg = jnp.where(grp[:, None] == grp[None, :], 1.0 / DK, 0.0).astype(BF16)

    cos_p, sin_p = _rope_tables(jnp.arange(seq, dtype=jnp.int32))
    cos_s, sin_s = _rope_tables(past + (jnp.arange(ms, dtype=jnp.int32) % t_new))


    ck = cache_k.reshape(depth, cache_k.shape[1], PAGE_SIZE * N_HEADS_A, LANES)
    cv = cache_v.reshape(depth, cache_v.shape[1], PAGE_SIZE * N_HEADS_A, LANES)
    new_pad = ((0, 0), (0, 0), (0, PAGE_SIZE - t_new), (0, 0))

    hp = x_prompt.reshape(b * seq, d)
    hs = x_sample.reshape(ms, d)
    kp_l, vp_l, poolp_l, ks_l, vs_l, pools_l, gv_l = [], [], [], [], [], [], []
    pad_rows = lambda x: jnp.pad(x.reshape(db, t_new, -1), ((0, 0), (0, CHUNK - t_new), (0, 0))
                                 ).reshape(db * CHUNK, -1)

    for l in range(depth):
        hp, hs = _ffn(hp, hs, n1, w1g, w1u, w1d, l, tm_ffn, tf)

        proj_out = _proj(hp, hs, nm, w_in_b, qg, kg, gg, cos_p, sin_p, cos_s, sin_s, avg, l,
                         tm_proj, seq // tm_proj)

        q, k, kb, v, vb, up, ug, vg = proj_out[:8]
        a = _attn_prompt(lam, q, kb, vb, sub, l, b, seq, t_attn, heads_per_step)
        hp = _post(hp, a, up, up, ug, vg, pool_w_b, ps, gmlp_ws, gb, w_out_b, l, tm_post,
                   seq // tm_post)
        kp_l.append(k)
        vp_l.append(v)
        poolp_l.append(up.reshape(b, seq, d_pool)[:, seq - POOL_BUF:])

        q, k, kb, v, vb, us, ug, vg = proj_out[8:]
        q5 = q.reshape(db, t_new, N_HEADS_A, 2, DK).transpose(0, 2, 1, 3, 4)
        comp = jnp.arange(2)
        qm = jnp.where(comp[None, None, :, None, None, None] == comp[None, None, None, None, :, None],
                       q5[:, :, None], jnp.zeros((), BF16))
        qm = qm.reshape(db, N_HEADS_A, 2 * t_new, LANES)
        knew = jnp.pad(kb.reshape(db, t_new, N_HEADS_A, LANES).transpose(0, 2, 1, 3), new_pad)
        vnew = jnp.pad(vb.reshape(db, t_new, N_HEADS_A, LANES).transpose(0, 2, 1, 3), new_pad)
        a = _attn_sample(page_table, lam, qm, knew, vnew, sub, ck, cv, l, g_pages, t_new)
        ext = jnp.concatenate([state_pool[l], us.reshape(db, t_new, d_pool)], axis=1)
        halo = jnp.pad(state_pool[l], ((0, 0), (HALO - POOL_BUF, 0), (0, 0))).reshape(db * HALO, d_pool)
        hs_pad = _post(pad_rows(hs), pad_rows(a.astype(BF16)), pad_rows(us), halo, pad_rows(ug),
                       pad_rows(vg), pool_w_b, ps, gmlp_ws, gb, w_out_b, l, CHUNK, None)
        hs = hs_pad.reshape(db, CHUNK, d)[:, :t_new].reshape(ms, d)
        ks_l.append(k)
        vs_l.append(v)
        pools_l.append(ext[:, t_new:])
        gv_l.append(vg)

        hp, hs = _ffn(hp, hs, n2, w2g, w2u, w2d, l, tm_ffn, tf)

    n_pg = seq // PAGE_SIZE
    return (
        hp.reshape(b, seq, d),
        hs.reshape(db, t_new, d),
        jnp.stack(kp_l).reshape(depth, b, n_pg, PAGE_SIZE, N_HEADS_A, 2 * DK),
        jnp.stack(vp_l).reshape(depth, b, n_pg, PAGE_SIZE, N_HEADS_A, DV),
        jnp.stack(poolp_l),
        jnp.stack(ks_l).reshape(depth, db, t_new, N_HEADS_A, 2 * DK),
        jnp.stack(vs_l).reshape(depth, db, t_new, N_HEADS_A, DV),
        jnp.stack(pools_l),
        jnp.stack(gv_l).reshape(depth, db, t_new, d_gmlp),
    )
```
